```python
import math
import jax, jax.numpy as jnp
from jax import lax
import numpy as np

D_MODEL = 1024
BATCH = 1
SEQ = 16384
DEPTH = 2

N_MIX_HEADS = 16
HEAD_DIM = D_MODEL // N_MIX_HEADS
N_HEADS_DIL = 6
DILATED_PAIRS = ((128, 1), (512, 4), (2048, 16))
N_HEADS_DIFF = 4
DIFF_QK_DIM = HEAD_DIM // 2
N_HEADS_SB = 6
D_MIX = (N_HEADS_DIL + N_HEADS_DIFF + N_HEADS_SB) * HEAD_DIM
IN_SIZES = (N_HEADS_DIL * HEAD_DIM,) * 3 + (N_HEADS_DIFF * 2 * DIFF_QK_DIM,) * 2 + (N_HEADS_DIFF * HEAD_DIM,) + (N_HEADS_SB * HEAD_DIM,) * 3
D_IN = sum(IN_SIZES)
D_FF = 2816
Q_BLOCK = 128
NORM_EPS = 1e-6
N_NORMS = 6

kernel_name = "hybrid_dilated_diff_stickbreaking_macaron"


def rmsnorm(x, g):
    xf = x.astype(jnp.float32)
    y = xf * lax.rsqrt(jnp.mean(xf * xf, axis=-1, keepdims=True) + NORM_EPS)
    return (y * g.astype(jnp.float32)).astype(x.dtype)


def swiglu(h, w_gate, w_up, w_down):
    return (jax.nn.silu(h @ w_gate) * (h @ w_up)) @ w_down


def alibi_slopes(n):
    return 2.0 ** (-8.0 * jnp.arange(1, n + 1, dtype=jnp.float32) / n)


def to_heads(t, n_heads):
    b, s, _ = t.shape
    return t.reshape(b, s, n_heads, -1).transpose(0, 2, 1, 3)


def from_heads(t):
    b, h, s, dh = t.shape
    return t.transpose(0, 2, 1, 3).reshape(b, s, h * dh)


def query_blocks(t):
    b, h, s, dh = t.shape
    return t.reshape(b, h, s // Q_BLOCK, Q_BLOCK, dh).transpose(2, 0, 1, 3, 4)


def merge_blocks(t):
    nq, b, h, q, dh = t.shape
    return t.transpose(1, 2, 0, 3, 4).reshape(b, h, nq * q, dh)


def dilated_branch(q, k, v, slopes, window, dilation):
    b, h, s, dh = q.shape
    n_back = window // dilation
    chunk = dilation * n_back
    sp = -(-s // chunk) * chunk
    seg = sp // dilation
    nb = seg // n_back

    def split(t):
        t = jnp.pad(t, ((0, 0), (0, 0), (0, sp - s), (0, 0)))
        t = t.reshape(b, h, seg, dilation, dh).transpose(0, 1, 3, 2, 4)
        return t.reshape(b, h, dilation, nb, n_back, dh)

    def with_prev(t):
        prev = jnp.concatenate([jnp.zeros_like(t[:, :, :, :1]), t[:, :, :, :-1]], axis=3)
        return jnp.concatenate([prev, t], axis=4)

    qb = split(q)
    kk = with_prev(split(k))
    vv = with_prev(split(v))
    sc = jnp.einsum('bhrnqd,bhrnkd->bhrnqk', qb, kk, preferred_element_type=jnp.float32)
    qi = jnp.arange(n_back)[:, None]
    ki = jnp.arange(2 * n_back)[None, :]
    dist = n_back + qi - ki
    band = (dist >= 0) & (dist <= n_back)
    first = (jnp.arange(nb) == 0)[:, None, None] & (ki < n_back)[None]
    valid = band[None] & ~first
    sc = sc - slopes[None, :, None, None, None, None] * (dilation * dist).astype(jnp.float32)
    sc = jnp.where(valid, sc, -jnp.inf)
    m = jnp.max(sc, axis=-1, keepdims=True)
    p = jnp.exp(sc - m)
    den = jnp.sum(p, axis=-1)
    o = jnp.einsum('bhrnqk,bhrnkd->bhrnqd', p, vv, preferred_element_type=jnp.float32) / den[..., None]
    lse = m[..., 0] + jnp.log(den)
    o = o.reshape(b, h, dilation, seg, dh).transpose(0, 1, 3, 2, 4).reshape(b, h, sp, dh)[:, :, :s]
    lse = lse.reshape(b, h, dilation, seg).transpose(0, 1, 3, 2).reshape(b, h, sp)[:, :, :s]
    return o, lse


def dilated_attention(q, k, v):
    slopes = alibi_slopes(q.shape[1])
    outs, lses = zip(*[dilated_branch(q, k, v, slopes, w, d) for (w, d) in DILATED_PAIRS])
    wts = jax.nn.softmax(jnp.stack(lses), axis=0)
    return jnp.sum(wts[..., None] * jnp.stack(outs), axis=0)


def differential_attention(q1, q2, k1, k2, v, lam):
    s = q1.shape[2]
    slopes = alibi_slopes(q1.shape[1])
    kpos = jnp.arange(s)
    starts = jnp.arange(s // Q_BLOCK) * Q_BLOCK

    def block(args):
        start, a, c = args
        rel = (start + jnp.arange(Q_BLOCK))[:, None] - kpos[None, :]
        bias = jnp.where(rel >= 0, -slopes[:, None, None] * rel.astype(jnp.float32), -jnp.inf)
        a1 = jax.nn.softmax(jnp.einsum('bhqd,bhkd->bhqk', a, k1, preferred_element_type=jnp.float32) + bias, axis=-1)
        a2 = jax.nn.softmax(jnp.einsum('bhqd,bhkd->bhqk', c, k2, preferred_element_type=jnp.float32) + bias, axis=-1)
        return jnp.einsum('bhqk,bhkd->bhqd', a1 - lam * a2, v, preferred_element_type=jnp.float32)

    return merge_blocks(lax.map(block, (starts, query_blocks(q1), query_blocks(q2))))


def stick_breaking_attention(q, k, v):
    s = q.shape[2]
    kpos = jnp.arange(s)
    starts = jnp.arange(s // Q_BLOCK) * Q_BLOCK

    def block(args):
        start, a = args
        valid = kpos[None, :] < (start + jnp.arange(Q_BLOCK))[:, None]
        z = jnp.einsum('bhqd,bhkd->bhqk', a, k, preferred_element_type=jnp.float32)
        log_fail = jnp.where(valid, jax.nn.log_sigmoid(-z), 0.0)
        after = lax.cumsum(log_fail, axis=3, reverse=True) - log_fail
        wts = jnp.where(valid, jnp.exp(jax.nn.log_sigmoid(z) + after), 0.0)
        return jnp.einsum('bhqk,bhkd->bhqd', wts, v, preferred_element_type=jnp.float32)

    return merge_blocks(lax.map(block, (starts, query_blocks(q))))


def token_mixer(h, w_in, w_out, lam_params, subln_gain, layer):
    proj = h @ w_in
    idx = [int(i) for i in np.cumsum(IN_SIZES)[:-1]]
    qa, ka, va, qb, kb, vb, qc, kc, vc = jnp.split(proj, idx, axis=-1)
    b, s, _ = h.shape
    o_a = dilated_attention(to_heads(qa, N_HEADS_DIL) * (HEAD_DIM ** -0.5), to_heads(ka, N_HEADS_DIL), to_heads(va, N_HEADS_DIL))
    qb = qb.reshape(b, s, N_HEADS_DIFF, 2, DIFF_QK_DIM).transpose(0, 2, 3, 1, 4) * (DIFF_QK_DIM ** -0.5)
    kb = kb.reshape(b, s, N_HEADS_DIFF, 2, DIFF_QK_DIM).transpose(0, 2, 3, 1, 4)
    lam_init = 0.8 - 0.6 * math.exp(-0.3 * layer)
    lam = jnp.exp(jnp.sum(lam_params[0] * lam_params[1])) - jnp.exp(jnp.sum(lam_params[2] * lam_params[3])) + lam_init
    o_b = differential_attention(qb[:, :, 0], qb[:, :, 1], kb[:, :, 0], kb[:, :, 1], to_heads(vb, N_HEADS_DIFF), lam.astype(jnp.float32))
    o_b = rmsnorm(o_b, subln_gain) * (1.0 - lam_init)
    o_c = stick_breaking_attention(to_heads(qc, N_HEADS_SB) * (HEAD_DIM ** -0.5), to_heads(kc, N_HEADS_SB), to_heads(vc, N_HEADS_SB))
    mixed = jnp.concatenate([from_heads(o_a), from_heads(o_b), from_heads(o_c)], axis=-1).astype(h.dtype)
    return mixed @ w_out


def setup_inputs(seed: int = 0) -> dict:
    key = jax.random.key(seed)
    ks = jax.random.split(key, 10)
    f32 = jnp.float32
    x = jax.random.normal(ks[0], (BATCH, SEQ, D_MODEL), f32)
    norm_gains = 1.0 + 0.02 * jax.random.normal(ks[1], (DEPTH, N_NORMS, D_MODEL), f32)
    w_ffn_gate = jax.random.normal(ks[2], (DEPTH, 2, D_MODEL, D_FF), f32) * D_MODEL ** -0.5
    w_ffn_up = jax.random.normal(ks[3], (DEPTH, 2, D_MODEL, D_FF), f32) * D_MODEL ** -0.5
    w_ffn_down = jax.random.normal(ks[4], (DEPTH, 2, D_FF, D_MODEL), f32) * D_FF ** -0.5
    w_in = jax.random.normal(ks[5], (DEPTH, D_MODEL, D_IN), f32) * D_MODEL ** -0.5
    w_out = jax.random.normal(ks[6], (DEPTH, D_MIX, D_MODEL), f32) * D_MIX ** -0.5
    diff_lambda = 0.1 * jax.random.normal(ks[7], (DEPTH, 4, DIFF_QK_DIM), f32)
    diff_subln_gain = 1.0 + 0.02 * jax.random.normal(ks[8], (DEPTH, HEAD_DIM), f32)
    return {"x": x, "norm_gains": norm_gains, "w_ffn_gate": w_ffn_gate, "w_ffn_up": w_ffn_up,
            "w_ffn_down": w_ffn_down, "w_in": w_in, "w_out": w_out,
            "diff_lambda": diff_lambda, "diff_subln_gain": diff_subln_gain}


def reference(x, norm_gains, w_ffn_gate, w_ffn_up, w_ffn_down, w_in, w_out, diff_lambda, diff_subln_gain):
    for layer in range(DEPTH):
        g = norm_gains[layer]
        h = rmsnorm(x, g[0])
        x = x + 0.5 * rmsnorm(swiglu(h, w_ffn_gate[layer, 0], w_ffn_up[layer, 0], w_ffn_down[layer, 0]), g[1])
        h = rmsnorm(x, g[2])
        x = x + rmsnorm(token_mixer(h, w_in[layer], w_out[layer], diff_lambda[layer], diff_subln_gain[layer], layer), g[3])
        h = rmsnorm(x, g[4])
        x = x + 0.5 * rmsnorm(swiglu(h, w_ffn_gate[layer, 1], w_ffn_up[layer, 1], w_ffn_down[layer, 1]), g[5])
    return x
```

```python
import functools
import math

import jax
import jax.numpy as jnp
from jax import lax
from jax.experimental import pallas as pl
from jax.experimental.pallas import tpu as pltpu

D_MODEL = 1024
D_FF = 2816
HEAD_DIM = 64
N_HEADS_DIL = 6
N_HEADS_DIFF = 4
N_HEADS_SB = 6
DIFF_QK_DIM = HEAD_DIM // 2
DILATED_PAIRS = ((128, 1), (512, 4), (2048, 16))
N_BACK = 128
NORM_EPS = 1e-6
IN_SIZES = (384, 384, 384, 256, 256, 256, 384, 384, 384)

LANES = 128
VMEM_LIMIT = 56 * 1024 * 1024

FFN_ROWS = 512
FFN_CHUNKS = 2
PROJ_ROWS = 512
DIL_CHUNK = 2048
ATT_TQ = 256
ATT_TKV = 256
NEG_BIG = -1e30

F32 = jnp.float32
BF16 = jnp.bfloat16


def _rms(x, g):
    return x * lax.rsqrt(jnp.mean(x * x, axis=-1, keepdims=True) + NORM_EPS) * g


def _params(*sem):
    return pltpu.CompilerParams(dimension_semantics=sem, vmem_limit_bytes=VMEM_LIMIT)


def _resident(shape):
    zeros = (0,) * len(shape)
    return pl.BlockSpec(shape, lambda *_: zeros, pipeline_mode=pl.Buffered(1))


def _ffn_body(x_ref, gpre_ref, gpost_ref, wg_ref, wu_ref, wd_ref, o_ref):
    x = x_ref[...]
    hb = _rms(x, gpre_ref[...]).astype(BF16)
    step = D_FF // FFN_CHUNKS
    y = None
    for c in range(FFN_CHUNKS):
        sl = slice(c * step, (c + 1) * step)
        g = jnp.dot(hb, wg_ref[:, sl], preferred_element_type=F32)
        u = jnp.dot(hb, wu_ref[:, sl], preferred_element_type=F32)
        a = (g / (1.0 + jnp.exp(-g)) * u).astype(BF16)
        part = jnp.dot(a, wd_ref[sl, :], preferred_element_type=F32)
        y = part if y is None else y + part
    o_ref[...] = x + 0.5 * _rms(y, gpost_ref[...])


def _ffn(x, gpre, gpost, wg, wu, wd):
    s = x.shape[0]
    row = pl.BlockSpec((FFN_ROWS, D_MODEL), lambda i: (i, 0))
    return pl.pallas_call(
        _ffn_body,
        grid=(s // FFN_ROWS,),
        in_specs=[row, _resident((1, D_MODEL)), _resident((1, D_MODEL)),
                  _resident((D_MODEL, D_FF)), _resident((D_MODEL, D_FF)), _resident((D_FF, D_MODEL))],
        out_specs=row,
        out_shape=jax.ShapeDtypeStruct((s, D_MODEL), F32),
        compiler_params=_params("parallel"),
        name="ffn",
    )(x, gpre, gpost, wg, wu, wd)


N_NAT = 384 * 3 + 256 + 384
N_TR = 256 + 256 + 384 + 384


def _inproj_body(x_ref, g_ref, wn_ref, wt_ref, dil_ref, kb_ref, kc_ref, qbt_ref, vbt_ref, qct_ref, vct_ref):
    hb = _rms(x_ref[...], g_ref[...]).astype(BF16)
    pn = jnp.dot(hb, wn_ref[...], preferred_element_type=F32)
    dil_ref[:, 0:384] = pn[:, 0:384] * (HEAD_DIM ** -0.5)
    dil_ref[:, 384:1152] = pn[:, 384:1152]
    kb_ref[...] = pn[:, 1152:1408].astype(BF16)
    kc_ref[...] = pn[:, 1408:1792].astype(BF16)
    pt = lax.dot_general(wt_ref[...], hb, (((1,), (1,)), ((), ())), preferred_element_type=F32)
    qbt_ref[...] = (pt[0:256] * (DIFF_QK_DIM ** -0.5)).astype(BF16)
    vbt_ref[...] = pt[256:512].astype(BF16)
    qct_ref[...] = (pt[512:896] * (HEAD_DIM ** -0.5)).astype(BF16)
    vct_ref[...] = pt[896:1280].astype(BF16)


def _inproj(x, g, wn, wt):
    s = x.shape[0]
    t = PROJ_ROWS
    nat = lambda w: pl.BlockSpec((t, w), lambda i: (i, 0))
    tr = lambda w: pl.BlockSpec((w, t), lambda i: (0, i))
    return pl.pallas_call(
        _inproj_body,
        grid=(s // t,),
        in_specs=[nat(D_MODEL), _resident((1, D_MODEL)), _resident((D_MODEL, N_NAT)), _resident((N_TR, D_MODEL))],
        out_specs=[nat(1152), nat(256), nat(384), tr(256), tr(256), tr(384), tr(384)],
        out_shape=[jax.ShapeDtypeStruct((s, 1152), F32),
                   jax.ShapeDtypeStruct((s, 256), BF16), jax.ShapeDtypeStruct((s, 384), BF16),
                   jax.ShapeDtypeStruct((256, s), BF16), jax.ShapeDtypeStruct((256, s), BF16),
                   jax.ShapeDtypeStruct((384, s), BF16), jax.ShapeDtypeStruct((384, s), BF16)],
        compiler_params=_params("parallel"),
        name="inproj",
    )(x, g, wn, wt)


def _dil_body(slope_ref, q_ref, kp_ref, kc_ref, vp_ref, vc_ref, o_ref, kbuf, vbuf, oacc, macc, lacc):
    n = pl.program_id(1)
    kbuf[0:DIL_CHUNK, :] = kp_ref[...]
    kbuf[DIL_CHUNK:, :] = kc_ref[...]
    vbuf[0:DIL_CHUNK, :] = vp_ref[...]
    vbuf[DIL_CHUNK:, :] = vc_ref[...]
    slope = slope_ref[0]
    lane = lax.broadcasted_iota(jnp.int32, (1, LANES), 1)
    head0 = lane < HEAD_DIM
    slope0 = jnp.max(jnp.where(head0, slope, 0.0), axis=1, keepdims=True)
    slope1 = jnp.max(jnp.where(head0, 0.0, slope), axis=1, keepdims=True)
    qi = lax.broadcasted_iota(jnp.int32, (N_BACK, 2 * N_BACK), 0)
    ki = lax.broadcasted_iota(jnp.int32, (N_BACK, 2 * N_BACK), 1)
    dist = N_BACK + qi - ki
    band = (dist >= 0) & (dist <= N_BACK)
    distf = dist.astype(F32)
    nt = (((1,), (1,)), ((), ()))

    for bi, (window, d) in enumerate(DILATED_PAIRS):
        per_chunk = DIL_CHUNK // window

        def tile(it, carry, d=d, window=window, per_chunk=per_chunk, bi=bi):
            if d == 1:
                c, r = it, 0
            elif per_chunk == 1:
                c, r = 0, it
            else:
                c, r = it // d, it % d
            q_start = c * window + r
            k_start = DIL_CHUNK + q_start - window
            if d == 1:
                rows_q = pl.ds(q_start, N_BACK)
                rows_k = pl.ds(k_start, 2 * N_BACK)
            else:
                rows_q = pl.ds(q_start, N_BACK, stride=d)
                rows_k = pl.ds(k_start, 2 * N_BACK, stride=d)
            q2 = q_ref[rows_q, :]
            k2 = kbuf[rows_k, :].astype(BF16)
            v2 = vbuf[rows_k, :].astype(BF16)
            first = jnp.logical_and(n == 0, c == 0)
            valid = band & jnp.logical_not(jnp.logical_and(first, ki < N_BACK))
            outs, ms, ls = [], [], []
            for e, (hm, sl) in enumerate(((head0, slope0), (jnp.logical_not(head0), slope1))):
                qe = jnp.where(hm, q2, 0.0).astype(BF16)
                sc = lax.dot_general(qe, k2, nt, preferred_element_type=F32)
                sc = jnp.where(valid, sc - (sl * float(d)) * distf, -jnp.inf)
                m = jnp.max(sc, axis=1, keepdims=True)
                p = jnp.exp(sc - m)
                ls.append(jnp.sum(p, axis=1, keepdims=True))
                ms.append(m)
                outs.append(jnp.dot(p.astype(BF16), v2, preferred_element_type=F32))
            o_blk = jnp.where(head0, outs[0], outs[1])
            m_blk = jnp.where(head0, ms[0], ms[1])
            l_blk = jnp.where(head0, ls[0], ls[1])
            if bi == 0:
                oacc[rows_q, :] = o_blk
                macc[rows_q, :] = m_blk
                lacc[rows_q, :] = l_blk
            else:
                m_old = macc[rows_q, :]
                m_new = jnp.maximum(m_old, m_blk)
                a = jnp.exp(m_old - m_new)
                b = jnp.exp(m_blk - m_new)
                oacc[rows_q, :] = a * oacc[rows_q, :] + b * o_blk
                lacc[rows_q, :] = a * lacc[rows_q, :] + b * l_blk
                macc[rows_q, :] = m_new
            return carry

        lax.fori_loop(0, DIL_CHUNK // N_BACK, tile, 0)

    o_ref[...] = (oacc[...] / lacc[...]).astype(BF16)


def _dilated(dil, slopes):
    s = dil.shape[0]
    npairs = N_HEADS_DIL // 2
    blk = lambda col0, prev: pl.BlockSpec(
        (DIL_CHUNK, LANES),
        (lambda p, n: (jnp.maximum(n - 1, 0), col0 + p)) if prev else (lambda p, n: (n, col0 + p)))
    buf = lambda rows: pltpu.VMEM((rows, LANES), F32)
    return pl.pallas_call(
        _dil_body,
        grid=(npairs, s // DIL_CHUNK),
        in_specs=[pl.BlockSpec((1, 1, LANES), lambda p, n: (p, 0, 0)),
                  blk(0, False), blk(3, True), blk(3, False), blk(6, True), blk(6, False)],
        out_specs=pl.BlockSpec((DIL_CHUNK, LANES), lambda p, n: (n, p)),
        out_shape=jax.ShapeDtypeStruct((s, N_HEADS_DIL * HEAD_DIM), BF16),
        scratch_shapes=[buf(2 * DIL_CHUNK), buf(2 * DIL_CHUNK), buf(DIL_CHUNK), buf(DIL_CHUNK), buf(DIL_CHUNK)],
        compiler_params=_params("parallel", "parallel"),
        name="dilated",
    )(slopes, dil, dil, dil, dil, dil)


def _diff_body(lam_init, slope_ref, lam_ref, gain_ref, qt_ref, k_ref, vt_ref, o_ref, m_ref, l_ref, acc_ref):
    tq, tkv = ATT_TQ, ATT_TKV
    i = pl.program_id(1)
    qt = qt_ref[...]
    row = lax.broadcasted_iota(jnp.int32, (LANES, tq), 0)
    zero = jnp.zeros_like(qt)
    qa = jnp.concatenate(
        [jnp.where((row >= DIFF_QK_DIM * c) & (row < DIFF_QK_DIM * (c + 1)), qt, zero) for c in range(4)], axis=1)
    slope = slope_ref[0]
    kpos = lax.broadcasted_iota(jnp.int32, (tkv, 4 * tq), 0)
    bias = kpos.astype(F32) * slope
    qpos = lax.broadcasted_iota(jnp.int32, (tkv, 4 * tq), 1) & (tq - 1)
    causal = kpos <= qpos

    m_ref[...] = jnp.full(m_ref.shape, NEG_BIG, F32)
    l_ref[...] = jnp.zeros(l_ref.shape, F32)
    acc_ref[...] = jnp.zeros(acc_ref.shape, F32)

    def tile(j, masked):
        ks = pl.multiple_of(j * tkv, tkv)
        k = k_ref[pl.ds(ks, tkv), :]
        s = jnp.dot(k, qa, preferred_element_type=F32) + bias
        if masked:
            s = jnp.where(causal, s, -jnp.inf)
        c = slope * (j * tkv - i * tq).astype(F32)
        m_old = m_ref[...]
        m_new = jnp.maximum(m_old, jnp.max(s, axis=0, keepdims=True) + c)
        p = jnp.exp(s - (m_new - c))
        alpha = jnp.exp(m_old - m_new)
        l_ref[...] = alpha * l_ref[...] + jnp.sum(p, axis=0, keepdims=True)
        m_ref[...] = m_new
        pb = p.astype(BF16)
        vt = vt_ref[:, pl.ds(ks, tkv)]
        h = HEAD_DIM
        acc_ref[0:h, :] = alpha[:, 0:2 * tq] * acc_ref[0:h, :] + jnp.dot(
            vt[0:h], pb[:, 0:2 * tq], preferred_element_type=F32)
        acc_ref[h:, :] = alpha[:, 2 * tq:] * acc_ref[h:, :] + jnp.dot(
            vt[h:], pb[:, 2 * tq:], preferred_element_type=F32)

    def full_tile(j, carry):
        tile(j, False)
        return carry

    lax.fori_loop(0, i, full_tile, 0)
    tile(i, True)

    lp = lam_ref[...]
    lam = (jnp.exp(jnp.sum(lp[0:1] * lp[1:2], axis=1, keepdims=True))
           - jnp.exp(jnp.sum(lp[2:3] * lp[3:4], axis=1, keepdims=True)) + lam_init)
    inv = 1.0 / l_ref[...]
    acc = acc_ref[...]
    outs = []
    for e in range(2):
        a = acc[e * HEAD_DIM:(e + 1) * HEAD_DIM]
        i1 = inv[:, (2 * e) * tq:(2 * e + 1) * tq]
        i2 = inv[:, (2 * e + 1) * tq:(2 * e + 2) * tq]
        o = a[:, 0:tq] * i1 - lam * (a[:, tq:] * i2)
        o = o * lax.rsqrt(jnp.mean(o * o, axis=0, keepdims=True) + NORM_EPS)
        outs.append(o * gain_ref[...] * (1.0 - lam_init))
    o_ref[...] = jnp.concatenate(outs, axis=0).T.astype(BF16)


def _diff_attention(qbt, kb, vbt, lam_params, gain_col, slopes, lam_init):
    s = kb.shape[0]
    tq = ATT_TQ
    npairs = N_HEADS_DIFF // 2
    return pl.pallas_call(
        functools.partial(_diff_body, lam_init),
        grid=(npairs, s // tq),
        in_specs=[pl.BlockSpec((1, 1, 4 * tq), lambda p, i: (p, 0, 0)),
                  pl.BlockSpec((4, DIFF_QK_DIM), lambda p, i: (0, 0)),
                  pl.BlockSpec((HEAD_DIM, 1), lambda p, i: (0, 0)),
                  pl.BlockSpec((LANES, tq), lambda p, i: (p, i)),
                  pl.BlockSpec((s, LANES), lambda p, i: (0, p)),
                  pl.BlockSpec((LANES, s), lambda p, i: (p, 0))],
        out_specs=pl.BlockSpec((tq, LANES), lambda p, i: (i, p)),
        out_shape=jax.ShapeDtypeStruct((s, N_HEADS_DIFF * HEAD_DIM), BF16),
        scratch_shapes=[pltpu.VMEM((1, 4 * tq), F32), pltpu.VMEM((1, 4 * tq), F32),
                        pltpu.VMEM((LANES, 2 * tq), F32)],
        compiler_params=_params("parallel", "arbitrary"),
        name="diff_attn",
    )(slopes, lam_params, gain_col, qbt, kb, vbt)


def _sb_body(qt_ref, k_ref, vt_ref, o_ref, carry_ref, acc_ref):
    tq, tkv = ATT_TQ, ATT_TKV
    i = pl.program_id(1)
    qt = qt_ref[...]
    row = lax.broadcasted_iota(jnp.int32, (LANES, tq), 0)
    zero = jnp.zeros_like(qt)
    qa = jnp.concatenate([jnp.where(row < HEAD_DIM, qt, zero), jnp.where(row >= HEAD_DIM, qt, zero)], axis=1)
    ur = lax.broadcasted_iota(jnp.int32, (tkv, tkv), 0)
    uc = lax.broadcasted_iota(jnp.int32, (tkv, tkv), 1)
    upper = (uc > ur).astype(BF16)
    kpos = lax.broadcasted_iota(jnp.int32, (tkv, 2 * tq), 0)
    qpos = lax.broadcasted_iota(jnp.int32, (tkv, 2 * tq), 1) & (tq - 1)
    causal = kpos < qpos

    carry_ref[...] = jnp.zeros(carry_ref.shape, F32)
    acc_ref[...] = jnp.zeros(acc_ref.shape, F32)

    def tile(j, masked):
        ks = pl.multiple_of(j * tkv, tkv)
        k = k_ref[pl.ds(ks, tkv), :]
        z = jnp.dot(k, qa, preferred_element_type=F32)
        sp = jnp.maximum(z, 0.0) + jnp.log1p(jnp.exp(-jnp.abs(z)))
        lf = -sp
        if masked:
            lf = jnp.where(causal, lf, 0.0)
        hi = lf.astype(BF16)
        lo = (lf - hi.astype(F32)).astype(BF16)
        cs = (jnp.dot(upper, hi, preferred_element_type=F32)
              + jnp.dot(upper, lo, preferred_element_type=F32))
        w = jnp.exp((z - sp) + (cs + carry_ref[...]))
        if masked:
            w = jnp.where(causal, w, 0.0)
        wb = w.astype(BF16)
        carry_ref[...] = carry_ref[...] + cs[0:1] + lf[0:1]
        vt = vt_ref[:, pl.ds(ks, tkv)]
        h = HEAD_DIM
        acc_ref[0:h, :] = acc_ref[0:h, :] + jnp.dot(vt[0:h], wb[:, 0:tq], preferred_element_type=F32)
        acc_ref[h:, :] = acc_ref[h:, :] + jnp.dot(vt[h:], wb[:, tq:], preferred_element_type=F32)

    tile(i, True)

    def full_tile(it, carry):
        tile(i - 1 - it, False)
        return carry

    lax.fori_loop(0, i, full_tile, 0)
    o_ref[...] = acc_ref[...].T.astype(BF16)


def _sb_attention(qct, kc, vct):
    s = kc.shape[0]
    tq = ATT_TQ
    npairs = N_HEADS_SB // 2
    return pl.pallas_call(
        _sb_body,
        grid=(npairs, s // tq),
        in_specs=[pl.BlockSpec((LANES, tq), lambda p, i: (p, i)),
                  pl.BlockSpec((s, LANES), lambda p, i: (0, p)),
                  pl.BlockSpec((LANES, s), lambda p, i: (p, 0))],
        out_specs=pl.BlockSpec((tq, LANES), lambda p, i: (i, p)),
        out_shape=jax.ShapeDtypeStruct((s, N_HEADS_SB * HEAD_DIM), BF16),
        scratch_shapes=[pltpu.VMEM((1, 2 * tq), F32), pltpu.VMEM((LANES, tq), F32)],
        compiler_params=_params("parallel", "arbitrary"),
        name="sb_attn",
    )(qct, kc, vct)


def _outproj_body(x_ref, g_ref, oa_ref, ob_ref, oc_ref, wa_ref, wb_ref, wc_ref, o_ref):
    y = (jnp.dot(oa_ref[...], wa_ref[...], preferred_element_type=F32)
         + jnp.dot(ob_ref[...], wb_ref[...], preferred_element_type=F32)
         + jnp.dot(oc_ref[...], wc_ref[...], preferred_element_type=F32))
    o_ref[...] = x_ref[...] + _rms(y, g_ref[...])


def _outproj(x, g, oa, ob, oc, wa, wb, wc):
    s = x.shape[0]
    t = PROJ_ROWS
    nat = lambda w: pl.BlockSpec((t, w), lambda i: (i, 0))
    return pl.pallas_call(
        _outproj_body,
        grid=(s // t,),
        in_specs=[nat(D_MODEL), _resident((1, D_MODEL)), nat(384), nat(256), nat(384),
                  _resident((384, D_MODEL)), _resident((256, D_MODEL)), _resident((384, D_MODEL))],
        out_specs=nat(D_MODEL),
        out_shape=jax.ShapeDtypeStruct((s, D_MODEL), F32),
        compiler_params=_params("parallel"),
        name="outproj",
    )(x, g, oa, ob, oc, wa, wb, wc)


def _alibi_slopes(n):
    return 2.0 ** (-8.0 * jnp.arange(1, n + 1, dtype=F32) / n)


def kernel(x, norm_gains, w_ffn_gate, w_ffn_up, w_ffn_down, w_in, w_out, diff_lambda, diff_subln_gain):
    b, s, d = x.shape
    assert b == 1 and d == D_MODEL and s % DIL_CHUNK == 0 and ATT_TQ == ATT_TKV
    depth = norm_gains.shape[0]
    xs = x.reshape(s, d)

    off = [0]
    for w in IN_SIZES:
        off.append(off[-1] + w)
    col = lambda a, i: a[:, off[i]:off[i + 1]]

    sl_dil = jnp.repeat(_alibi_slopes(N_HEADS_DIL), HEAD_DIM).reshape(N_HEADS_DIL // 2, 1, LANES)
    sl_diff = jnp.repeat(_alibi_slopes(N_HEADS_DIFF), 2 * ATT_TQ).reshape(N_HEADS_DIFF // 2, 1, 4 * ATT_TQ)

    for layer in range(depth):
        g = norm_gains[layer].reshape(-1, 1, D_MODEL)
        wg = w_ffn_gate[layer].astype(BF16)
        wu = w_ffn_up[layer].astype(BF16)
        wd = w_ffn_down[layer].astype(BF16)
        wi = w_in[layer]
        wn = jnp.concatenate([col(wi, 0), col(wi, 1), col(wi, 2), col(wi, 4), col(wi, 7)], axis=1).astype(BF16)
        wt = jnp.concatenate([col(wi, 3), col(wi, 5), col(wi, 6), col(wi, 8)], axis=1).T.astype(BF16)
        wo = w_out[layer].astype(BF16)
        lam_init = 0.8 - 0.6 * math.exp(-0.3 * layer)

        xs = _ffn(xs, g[0], g[1], wg[0], wu[0], wd[0])
        dil, kb, kc, qbt, vbt, qct, vct = _inproj(xs, g[2], wn, wt)
        oa = _dilated(dil, sl_dil)
        ob = _diff_attention(qbt, kb, vbt, diff_lambda[layer], diff_subln_gain[layer].reshape(HEAD_DIM, 1),
                             sl_diff, lam_init)
        oc = _sb_attention(qct, kc, vct)
        xs = _outproj(xs, g[3], oa, ob, oc, wo[0:384], wo[384:640], wo[640:1024])
        xs = _ffn(xs, g[4], g[5], wg[1], wu[1], wd[1])
    return xs.reshape(b, s, d)
```

```python
import functools
import math

import jax
import jax.numpy as jnp
from jax import lax
from jax.experimental import pallas as pl
from jax.experimental.pallas import tpu as pltpu

D_MODEL = 1024
D_FF = 2816
HEAD_DIM = 64
N_HEADS_DIL = 6
N_HEADS_DIFF = 4
N_HEADS_SB = 6
DIFF_QK_DIM = HEAD_DIM // 2
DILATED_PAIRS = ((128, 1), (512, 4), (2048, 16))
N_BACK = 128
NORM_EPS = 1e-6
IN_SIZES = (384, 384, 384, 256, 256, 256, 384, 384, 384)

LANES = 128
VMEM_LIMIT = 56 * 1024 * 1024

FFN_ROWS = 512
FFN_CHUNKS = 2
PROJ_ROWS = 512
DIL_CHUNK = 2048
ATT_TQ = 256
ATT_TKV = 256
NEG_BIG = -1e30
LOG2E = 1.4426950408889634
SB_DEAD = -160.0

F32 = jnp.float32
BF16 = jnp.bfloat16


def _rms(x, g):
    return x * lax.rsqrt(jnp.mean(x * x, axis=-1, keepdims=True) + NORM_EPS) * g


def _params(*sem):
    return pltpu.CompilerParams(dimension_semantics=sem, vmem_limit_bytes=VMEM_LIMIT)


def _resident(shape):
    zeros = (0,) * len(shape)
    return pl.BlockSpec(shape, lambda *_: zeros, pipeline_mode=pl.Buffered(1))


def _ffn_body(x_ref, gpre_ref, gpost_ref, wg_ref, wu_ref, wd_ref, o_ref):
    x = x_ref[...]
    hb = _rms(x, gpre_ref[...]).astype(BF16)
    step = D_FF // FFN_CHUNKS
    y = None
    for c in range(FFN_CHUNKS):
        sl = slice(c * step, (c + 1) * step)
        g = jnp.dot(hb, wg_ref[:, sl], preferred_element_type=F32)
        u = jnp.dot(hb, wu_ref[:, sl], preferred_element_type=F32)
        a = (g / (1.0 + jnp.exp(-g)) * u).astype(BF16)
        part = jnp.dot(a, wd_ref[sl, :], preferred_element_type=F32)
        y = part if y is None else y + part
    o_ref[...] = x + 0.5 * _rms(y, gpost_ref[...])


def _ffn(x, gpre, gpost, wg, wu, wd):
    s = x.shape[0]
    row = pl.BlockSpec((FFN_ROWS, D_MODEL), lambda i: (i, 0))
    return pl.pallas_call(
        _ffn_body,
        grid=(s // FFN_ROWS,),
        in_specs=[row, _resident((1, D_MODEL)), _resident((1, D_MODEL)),
                  _resident((D_MODEL, D_FF)), _resident((D_MODEL, D_FF)), _resident((D_FF, D_MODEL))],
        out_specs=row,
        out_shape=jax.ShapeDtypeStruct((s, D_MODEL), F32),
        compiler_params=_params("parallel"),
        name="ffn",
    )(x, gpre, gpost, wg, wu, wd)


N_NAT = 384 * 3 + 256 + 384
N_TR = 256 + 256 + 384 + 384


def _inproj_body(x_ref, g_ref, wn_ref, wt_ref, dil_ref, kb_ref, kc_ref, qbt_ref, vbt_ref, qct_ref, vct_ref):
    hb = _rms(x_ref[...], g_ref[...]).astype(BF16)
    pn = jnp.dot(hb, wn_ref[...], preferred_element_type=F32)
    dil_ref[:, 0:384] = pn[:, 0:384] * (HEAD_DIM ** -0.5)
    dil_ref[:, 384:1152] = pn[:, 384:1152]
    kb_ref[...] = pn[:, 1152:1408].astype(BF16)
    kc_ref[...] = pn[:, 1408:1792].astype(BF16)
    pt = lax.dot_general(wt_ref[...], hb, (((1,), (1,)), ((), ())), preferred_element_type=F32)
    qbt_ref[...] = (pt[0:256] * (DIFF_QK_DIM ** -0.5)).astype(BF16)
    vbt_ref[...] = pt[256:512].astype(BF16)
    qct_ref[...] = (pt[512:896] * (HEAD_DIM ** -0.5 * LOG2E)).astype(BF16)
    vct_ref[...] = pt[896:1280].astype(BF16)


def _inproj(x, g, wn, wt):
    s = x.shape[0]
    t = PROJ_ROWS
    nat = lambda w: pl.BlockSpec((t, w), lambda i: (i, 0))
    tr = lambda w: pl.BlockSpec((w, t), lambda i: (0, i))
    return pl.pallas_call(
        _inproj_body,
        grid=(s // t,),
        in_specs=[nat(D_MODEL), _resident((1, D_MODEL)), _resident((D_MODEL, N_NAT)), _resident((N_TR, D_MODEL))],
        out_specs=[nat(1152), nat(256), nat(384), tr(256), tr(256), tr(384), tr(384)],
        out_shape=[jax.ShapeDtypeStruct((s, 1152), F32),
                   jax.ShapeDtypeStruct((s, 256), BF16), jax.ShapeDtypeStruct((s, 384), BF16),
                   jax.ShapeDtypeStruct((256, s), BF16), jax.ShapeDtypeStruct((256, s), BF16),
                   jax.ShapeDtypeStruct((384, s), BF16), jax.ShapeDtypeStruct((384, s), BF16)],
        compiler_params=_params("parallel"),
        name="inproj",
    )(x, g, wn, wt)


def _dil_body(slope_ref, q_ref, kp_ref, kc_ref, vp_ref, vc_ref, o_ref, kbuf, vbuf, oacc, macc, lacc):
    n = pl.program_id(1)
    kbuf[0:DIL_CHUNK, :] = kp_ref[...]
    kbuf[DIL_CHUNK:, :] = kc_ref[...]
    vbuf[0:DIL_CHUNK, :] = vp_ref[...]
    vbuf[DIL_CHUNK:, :] = vc_ref[...]
    slope = slope_ref[0]
    lane = lax.broadcasted_iota(jnp.int32, (1, LANES), 1)
    head0 = lane < HEAD_DIM
    slope0 = jnp.max(jnp.where(head0, slope, 0.0), axis=1, keepdims=True)
    slope1 = jnp.max(jnp.where(head0, 0.0, slope), axis=1, keepdims=True)
    qi = lax.broadcasted_iota(jnp.int32, (N_BACK, 2 * N_BACK), 0)
    ki = lax.broadcasted_iota(jnp.int32, (N_BACK, 2 * N_BACK), 1)
    dist = N_BACK + qi - ki
    band = (dist >= 0) & (dist <= N_BACK)
    distf = dist.astype(F32)
    nt = (((1,), (1,)), ((), ()))

    for bi, (window, d) in enumerate(DILATED_PAIRS):
        per_chunk = DIL_CHUNK // window

        def tile(it, carry, d=d, window=window, per_chunk=per_chunk, bi=bi):
            if d == 1:
                c, r = it, 0
            elif per_chunk == 1:
                c, r = 0, it
            else:
                c, r = it // d, it % d
            q_start = c * window + r
            k_start = DIL_CHUNK + q_start - window
            if d == 1:
                rows_q = pl.ds(q_start, N_BACK)
                rows_k = pl.ds(k_start, 2 * N_BACK)
            else:
                rows_q = pl.ds(q_start, N_BACK, stride=d)
                rows_k = pl.ds(k_start, 2 * N_BACK, stride=d)
            q2 = q_ref[rows_q, :]
            k2 = kbuf[rows_k, :].astype(BF16)
            v2 = vbuf[rows_k, :].astype(BF16)
            first = jnp.logical_and(n == 0, c == 0)
            valid = band & jnp.logical_not(jnp.logical_and(first, ki < N_BACK))
            outs, ms, ls = [], [], []
            for e, (hm, sl) in enumerate(((head0, slope0), (jnp.logical_not(head0), slope1))):
                qe = jnp.where(hm, q2, 0.0).astype(BF16)
                sc = lax.dot_general(qe, k2, nt, preferred_element_type=F32)
                sc = jnp.where(valid, sc - (sl * float(d)) * distf, -jnp.inf)
                m = jnp.max(sc, axis=1, keepdims=True)
                p = jnp.exp(sc - m)
                ls.append(jnp.sum(p, axis=1, keepdims=True))
                ms.append(m)
                outs.append(jnp.dot(p.astype(BF16), v2, preferred_element_type=F32))
            o_blk = jnp.where(head0, outs[0], outs[1])
            m_blk = jnp.where(head0, ms[0], ms[1])
            l_blk = jnp.where(head0, ls[0], ls[1])
            if bi == 0:
                oacc[rows_q, :] = o_blk
                macc[rows_q, :] = m_blk
                lacc[rows_q, :] = l_blk
            else:
                m_old = macc[rows_q, :]
                m_new = jnp.maximum(m_old, m_blk)
                a = jnp.exp(m_old - m_new)
                b = jnp.exp(m_blk - m_new)
                oacc[rows_q, :] = a * oacc[rows_q, :] + b * o_blk
                lacc[rows_q, :] = a * lacc[rows_q, :] + b * l_blk
                macc[rows_q, :] = m_new
            return carry

        lax.fori_loop(0, DIL_CHUNK // N_BACK, tile, 0)

    o_ref[...] = (oacc[...] / lacc[...]).astype(BF16)


def _dilated(dil, slopes):
    s = dil.shape[0]
    npairs = N_HEADS_DIL // 2
    blk = lambda col0, prev: pl.BlockSpec(
        (DIL_CHUNK, LANES),
        (lambda p, n: (jnp.maximum(n - 1, 0), col0 + p)) if prev else (lambda p, n: (n, col0 + p)))
    buf = lambda rows: pltpu.VMEM((rows, LANES), F32)
    return pl.pallas_call(
        _dil_body,
        grid=(npairs, s // DIL_CHUNK),
        in_specs=[pl.BlockSpec((1, 1, LANES), lambda p, n: (p, 0, 0)),
                  blk(0, False), blk(3, True), blk(3, False), blk(6, True), blk(6, False)],
        out_specs=pl.BlockSpec((DIL_CHUNK, LANES), lambda p, n: (n, p)),
        out_shape=jax.ShapeDtypeStruct((s, N_HEADS_DIL * HEAD_DIM), BF16),
        scratch_shapes=[buf(2 * DIL_CHUNK), buf(2 * DIL_CHUNK), buf(DIL_CHUNK), buf(DIL_CHUNK), buf(DIL_CHUNK)],
        compiler_params=_params("parallel", "parallel"),
        name="dilated",
    )(slopes, dil, dil, dil, dil, dil)


def _diff_body(lam_init, slope_ref, lam_ref, gain_ref, qt_ref, k_ref, vt_ref, o_ref, m_ref, l_ref, acc_ref):
    tq, tkv = ATT_TQ, ATT_TKV
    i = pl.program_id(1)
    qt = qt_ref[...]
    row = lax.broadcasted_iota(jnp.int32, (LANES, tq), 0)
    zero = jnp.zeros_like(qt)
    qa = jnp.concatenate(
        [jnp.where((row >= DIFF_QK_DIM * c) & (row < DIFF_QK_DIM * (c + 1)), qt, zero) for c in range(4)], axis=1)
    slope = slope_ref[0]
    kpos = lax.broadcasted_iota(jnp.int32, (tkv, 4 * tq), 0)
    bias = kpos.astype(F32) * slope
    qpos = lax.broadcasted_iota(jnp.int32, (tkv, 4 * tq), 1) & (tq - 1)
    causal = kpos <= qpos

    m_ref[...] = jnp.full(m_ref.shape, NEG_BIG, F32)
    l_ref[...] = jnp.zeros(l_ref.shape, F32)
    acc_ref[...] = jnp.zeros(acc_ref.shape, F32)

    def tile(j, masked):
        ks = pl.multiple_of(j * tkv, tkv)
        k = k_ref[pl.ds(ks, tkv), :]
        s = jnp.dot(k, qa, preferred_element_type=F32) + bias
        if masked:
            s = jnp.where(causal, s, -jnp.inf)
        c = slope * (j * tkv - i * tq).astype(F32)
        m_old = m_ref[...]
        m_new = jnp.maximum(m_old, jnp.max(s, axis=0, keepdims=True) + c)
        p = jnp.exp(s - (m_new - c))
        alpha = jnp.exp(m_old - m_new)
        l_ref[...] = alpha * l_ref[...] + jnp.sum(p, axis=0, keepdims=True)
        m_ref[...] = m_new
        pb = p.astype(BF16)
        vt = vt_ref[:, pl.ds(ks, tkv)]
        h = HEAD_DIM
        acc_ref[0:h, :] = alpha[:, 0:2 * tq] * acc_ref[0:h, :] + jnp.dot(
            vt[0:h], pb[:, 0:2 * tq], preferred_element_type=F32)
        acc_ref[h:, :] = alpha[:, 2 * tq:] * acc_ref[h:, :] + jnp.dot(
            vt[h:], pb[:, 2 * tq:], preferred_element_type=F32)

    def full_tile(j, carry):
        tile(j, False)
        return carry

    lax.fori_loop(0, i, full_tile, 0)
    tile(i, True)

    lp = lam_ref[...]
    lam = (jnp.exp(jnp.sum(lp[0:1] * lp[1:2], axis=1, keepdims=True))
           - jnp.exp(jnp.sum(lp[2:3] * lp[3:4], axis=1, keepdims=True)) + lam_init)
    inv = 1.0 / l_ref[...]
    acc = acc_ref[...]
    outs = []
    for e in range(2):
        a = acc[e * HEAD_DIM:(e + 1) * HEAD_DIM]
        i1 = inv[:, (2 * e) * tq:(2 * e + 1) * tq]
        i2 = inv[:, (2 * e + 1) * tq:(2 * e + 2) * tq]
        o = a[:, 0:tq] * i1 - lam * (a[:, tq:] * i2)
        o = o * lax.rsqrt(jnp.mean(o * o, axis=0, keepdims=True) + NORM_EPS)
        outs.append(o * gain_ref[...] * (1.0 - lam_init))
    o_ref[...] = jnp.concatenate(outs, axis=0).T.astype(BF16)


def _diff_attention(qbt, kb, vbt, lam_params, gain_col, slopes, lam_init):
    s = kb.shape[0]
    tq = ATT_TQ
    npairs = N_HEADS_DIFF // 2
    return pl.pallas_call(
        functools.partial(_diff_body, lam_init),
        grid=(npairs, s // tq),
        in_specs=[pl.BlockSpec((1, 1, 4 * tq), lambda p, i: (p, 0, 0)),
                  pl.BlockSpec((4, DIFF_QK_DIM), lambda p, i: (0, 0)),
                  pl.BlockSpec((HEAD_DIM, 1), lambda p, i: (0, 0)),
                  pl.BlockSpec((LANES, tq), lambda p, i: (p, i)),
                  pl.BlockSpec((s, LANES), lambda p, i: (0, p)),
                  pl.BlockSpec((LANES, s), lambda p, i: (p, 0))],
        out_specs=pl.BlockSpec((tq, LANES), lambda p, i: (i, p)),
        out_shape=jax.ShapeDtypeStruct((s, N_HEADS_DIFF * HEAD_DIM), BF16),
        scratch_shapes=[pltpu.VMEM((1, 4 * tq), F32), pltpu.VMEM((1, 4 * tq), F32),
                        pltpu.VMEM((LANES, 2 * tq), F32)],
        compiler_params=_params("parallel", "arbitrary"),
        name="diff_attn",
    )(slopes, lam_params, gain_col, qbt, kb, vbt)


def _sb_body(qt_ref, k_ref, vt_ref, o_ref, carry_ref, acc_ref):
    tq, tkv = ATT_TQ, ATT_TKV
    i = pl.program_id(1)
    qt = qt_ref[...]
    row = lax.broadcasted_iota(jnp.int32, (LANES, tq), 0)
    zero = jnp.zeros_like(qt)
    qa = jnp.concatenate([jnp.where(row < HEAD_DIM, qt, zero), jnp.where(row >= HEAD_DIM, qt, zero)], axis=1)
    ur = lax.broadcasted_iota(jnp.int32, (tkv, tkv), 0)
    uc = lax.broadcasted_iota(jnp.int32, (tkv, tkv), 1)
    upper = (uc >= ur).astype(BF16)
    kpos = lax.broadcasted_iota(jnp.int32, (tkv, 2 * tq), 0)
    qpos = lax.broadcasted_iota(jnp.int32, (tkv, 2 * tq), 1) & (tq - 1)
    causal = kpos < qpos

    carry_ref[...] = jnp.zeros(carry_ref.shape, F32)
    acc_ref[...] = jnp.zeros(acc_ref.shape, F32)

    def tile(j, masked):
        ks = pl.multiple_of(j * tkv, tkv)
        k = k_ref[pl.ds(ks, tkv), :]
        z = jnp.dot(k, qa, preferred_element_type=F32)
        sp = jnp.maximum(z, 0.0) + jnp.log(1.0 + jnp.exp2(-jnp.abs(z))) * LOG2E
        if masked:
            sp = jnp.where(causal, sp, 0.0)
        hi = sp.astype(BF16)
        lo = (sp - hi.astype(F32)).astype(BF16)
        cs = (jnp.dot(upper, hi, preferred_element_type=F32)
              + jnp.dot(upper, lo, preferred_element_type=F32))
        w = jnp.exp2((z + carry_ref[...]) - cs)
        if masked:
            w = jnp.where(causal, w, 0.0)
        wb = w.astype(BF16)
        carry_ref[...] = carry_ref[...] - cs[0:1]
        vt = vt_ref[:, pl.ds(ks, tkv)]
        h = HEAD_DIM
        acc_ref[0:h, :] = acc_ref[0:h, :] + jnp.dot(vt[0:h], wb[:, 0:tq], preferred_element_type=F32)
        acc_ref[h:, :] = acc_ref[h:, :] + jnp.dot(vt[h:], wb[:, tq:], preferred_element_type=F32)

    def alive():
        return jnp.max(carry_ref[...]) > SB_DEAD

    tile(i, True)

    def more(state):
        it, live = state
        return jnp.logical_and(it < i, live)

    def full_tile(state):
        it, _ = state
        tile(i - 1 - it, False)
        return it + 1, alive()

    lax.while_loop(more, full_tile, (0, alive()))
    o_ref[...] = acc_ref[...].T.astype(BF16)


def _sb_attention(qct, kc, vct):
    s = kc.shape[0]
    tq = ATT_TQ
    npairs = N_HEADS_SB // 2
    return pl.pallas_call(
        _sb_body,
        grid=(npairs, s // tq),
        in_specs=[pl.BlockSpec((LANES, tq), lambda p, i: (p, i)),
                  pl.BlockSpec((s, LANES), lambda p, i: (0, p)),
                  pl.BlockSpec((LANES, s), lambda p, i: (p, 0))],
        out_specs=pl.BlockSpec((tq, LANES), lambda p, i: (i, p)),
        out_shape=jax.ShapeDtypeStruct((s, N_HEADS_SB * HEAD_DIM), BF16),
        scratch_shapes=[pltpu.VMEM((1, 2 * tq), F32), pltpu.VMEM((LANES, tq), F32)],
        compiler_params=_params("parallel", "arbitrary"),
        name="sb_attn",
    )(qct, kc, vct)


def _outproj_body(x_ref, g_ref, oa_ref, ob_ref, oc_ref, wa_ref, wb_ref, wc_ref, o_ref):
    y = (jnp.dot(oa_ref[...], wa_ref[...], preferred_element_type=F32)
         + jnp.dot(ob_ref[...], wb_ref[...], preferred_element_type=F32)
         + jnp.dot(oc_ref[...], wc_ref[...], preferred_element_type=F32))
    o_ref[...] = x_ref[...] + _rms(y, g_ref[...])


def _outproj(x, g, oa, ob, oc, wa, wb, wc):
    s = x.shape[0]
    t = PROJ_ROWS
    nat = lambda w: pl.BlockSpec((t, w), lambda i: (i, 0))
    return pl.pallas_call(
        _outproj_body,
        grid=(s // t,),
        in_specs=[nat(D_MODEL), _resident((1, D_MODEL)), nat(384), nat(256), nat(384),
                  _resident((384, D_MODEL)), _resident((256, D_MODEL)), _resident((384, D_MODEL))],
        out_specs=nat(D_MODEL),
        out_shape=jax.ShapeDtypeStruct((s, D_MODEL), F32),
        compiler_params=_params("parallel"),
        name="outproj",
    )(x, g, oa, ob, oc, wa, wb, wc)


def _alibi_slopes(n):
    return 2.0 ** (-8.0 * jnp.arange(1, n + 1, dtype=F32) / n)


def kernel(x, norm_gains, w_ffn_gate, w_ffn_up, w_ffn_down, w_in, w_out, diff_lambda, diff_subln_gain):
    b, s, d = x.shape
    assert b == 1 and d == D_MODEL and s % DIL_CHUNK == 0 and ATT_TQ == ATT_TKV
    depth = norm_gains.shape[0]
    xs = x.reshape(s, d)

    off = [0]
    for w in IN_SIZES:
        off.append(off[-1] + w)
    col = lambda a, i: a[:, off[i]:off[i + 1]]

    sl_dil = jnp.repeat(_alibi_slopes(N_HEADS_DIL), HEAD_DIM).reshape(N_HEADS_DIL // 2, 1, LANES)
    sl_diff = jnp.repeat(_alibi_slopes(N_HEADS_DIFF), 2 * ATT_TQ).reshape(N_HEADS_DIFF // 2, 1, 4 * ATT_TQ)

    for layer in range(depth):
        g = norm_gains[layer].reshape(-1, 1, D_MODEL)
        wg = w_ffn_gate[layer].astype(BF16)
        wu = w_ffn_up[layer].astype(BF16)
        wd = w_ffn_down[layer].astype(BF16)
        wi = w_in[layer]
        wn = jnp.concatenate([col(wi, 0), col(wi, 1), col(wi, 2), col(wi, 4), col(wi, 7)], axis=1).astype(BF16)
        wt = jnp.concatenate([col(wi, 3), col(wi, 5), col(wi, 6), col(wi, 8)], axis=1).T.astype(BF16)
        wo = w_out[layer].astype(BF16)
        lam_init = 0.8 - 0.6 * math.exp(-0.3 * layer)

        xs = _ffn(xs, g[0], g[1], wg[0], wu[0], wd[0])
        dil, kb, kc, qbt, vbt, qct, vct = _inproj(xs, g[2], wn, wt)
        oa = _dilated(dil, sl_dil)
        ob = _diff_attention(qbt, kb, vbt, diff_lambda[layer], diff_subln_gain[layer].reshape(HEAD_DIM, 1),
                             sl_diff, lam_init)
        oc = _sb_attention(qct, kc, vct)
        xs = _outproj(xs, g[3], oa, ob, oc, wo[0:384], wo[384:640], wo[640:1024])
        xs = _ffn(xs, g[4], g[5], wg[1], wu[1], wd[1])
    return xs.reshape(b, s, d)
```

```python
import functools
import math

import jax
import jax.numpy as jnp
from jax import lax
from jax.experimental import pallas as pl
from jax.experimental.pallas import tpu as pltpu

D_MODEL = 1024
D_FF = 2816
HEAD_DIM = 64
N_HEADS_DIL = 6
N_HEADS_DIFF = 4
N_HEADS_SB = 6
DIFF_QK_DIM = HEAD_DIM // 2
DILATED_PAIRS = ((128, 1), (512, 4), (2048, 16))
N_BACK = 128
NORM_EPS = 1e-6
IN_SIZES = (384, 384, 384, 256, 256, 256, 384, 384, 384)

LANES = 128
SUBLANES = 8
VMEM_LIMIT = 56 * 1024 * 1024

FFN_ROWS = 512
FFN_CHUNKS = 2
PROJ_ROWS = 512
DIL_CHUNK = 2048
DIL_UNROLL = 4
ATT_TQ = 256
ATT_TKV = 256
NEG_BIG = -1e30
LOG2E = 1.4426950408889634
DEAD_EXP2 = -160.0

F32 = jnp.float32
BF16 = jnp.bfloat16


def _rms(x, g):
    return x * lax.rsqrt(jnp.mean(x * x, axis=-1, keepdims=True) + NORM_EPS) * g


def _params(*sem):
    return pltpu.CompilerParams(dimension_semantics=sem, vmem_limit_bytes=VMEM_LIMIT)


def _resident(shape):
    zeros = (0,) * len(shape)
    return pl.BlockSpec(shape, lambda *_: zeros, pipeline_mode=pl.Buffered(1))


def _ffn_body(x_ref, gpre_ref, gpost_ref, wg_ref, wu_ref, wd_ref, o_ref):
    x = x_ref[...]
    hb = _rms(x, gpre_ref[...]).astype(BF16)
    step = D_FF // FFN_CHUNKS
    y = None
    for c in range(FFN_CHUNKS):
        sl = slice(c * step, (c + 1) * step)
        g = jnp.dot(hb, wg_ref[:, sl], preferred_element_type=F32)
        u = jnp.dot(hb, wu_ref[:, sl], preferred_element_type=F32)
        a = (g / (1.0 + jnp.exp(-g)) * u).astype(BF16)
        part = jnp.dot(a, wd_ref[sl, :], preferred_element_type=F32)
        y = part if y is None else y + part
    o_ref[...] = x + 0.5 * _rms(y, gpost_ref[...])


def _ffn(x, gpre, gpost, wg, wu, wd):
    s = x.shape[0]
    row = pl.BlockSpec((FFN_ROWS, D_MODEL), lambda i: (i, 0))
    return pl.pallas_call(
        _ffn_body,
        grid=(s // FFN_ROWS,),
        in_specs=[row, _resident((1, D_MODEL)), _resident((1, D_MODEL)),
                  _resident((D_MODEL, D_FF)), _resident((D_MODEL, D_FF)), _resident((D_FF, D_MODEL))],
        out_specs=row,
        out_shape=jax.ShapeDtypeStruct((s, D_MODEL), F32),
        compiler_params=_params("parallel"),
        name="ffn",
    )(x, gpre, gpost, wg, wu, wd)


N_NAT = 384 * 3 + 256 + 384
N_TR = 256 + 256 + 384 + 384


def _inproj_body(x_ref, g_ref, wn_ref, wt_ref, dil_ref, kb_ref, kc_ref, qbt_ref, vbt_ref, qct_ref, vct_ref,
                 kn_ref, run_ref):
    hb = _rms(x_ref[...], g_ref[...]).astype(BF16)
    pn = jnp.dot(hb, wn_ref[...], preferred_element_type=F32)
    dil_ref[:, 0:384] = pn[:, 0:384] * (HEAD_DIM ** -0.5)
    dil_ref[:, 384:1152] = pn[:, 384:1152]
    kbf = pn[:, 1152:1408].astype(BF16)
    kb_ref[...] = kbf
    kc_ref[...] = pn[:, 1408:1792].astype(BF16)
    pt = lax.dot_general(wt_ref[...], hb, (((1,), (1,)), ((), ())), preferred_element_type=F32)
    qbt_ref[...] = (pt[0:256] * (DIFF_QK_DIM ** -0.5 * LOG2E)).astype(BF16)
    vbt_ref[...] = pt[256:512].astype(BF16)
    qct_ref[...] = (pt[512:896] * (HEAD_DIM ** -0.5 * LOG2E)).astype(BF16)
    vct_ref[...] = pt[896:1280].astype(BF16)

    @pl.when(pl.program_id(0) == 0)
    def _():
        run_ref[...] = jnp.zeros(run_ref.shape, F32)

    k2 = kbf.astype(F32)
    k2 = k2 * k2
    n2 = [jnp.sum(k2[:, p * LANES:(p + 1) * LANES], axis=1, keepdims=True) for p in range(N_HEADS_DIFF // 2)]
    lane = lax.broadcasted_iota(jnp.int32, (1, LANES), 1)
    rowi = lax.broadcasted_iota(jnp.int32, (SUBLANES, LANES), 0)
    run = run_ref[...]
    out = jnp.zeros((SUBLANES, LANES), F32)
    for r in range(PROJ_ROWS // ATT_TKV):
        tmax = [jnp.max(n[r * ATT_TKV:(r + 1) * ATT_TKV], axis=0, keepdims=True) for n in n2]
        run = jnp.maximum(run, jnp.where(lane == 0, tmax[0], jnp.where(lane == 1, tmax[1], 0.0)))
        out = jnp.where(rowi == r, jnp.sqrt(run), out)
    run_ref[...] = run
    kn_ref[0] = out


def _inproj(x, g, wn, wt):
    s = x.shape[0]
    t = PROJ_ROWS
    nat = lambda w: pl.BlockSpec((t, w), lambda i: (i, 0))
    tr = lambda w: pl.BlockSpec((w, t), lambda i: (0, i))
    return pl.pallas_call(
        _inproj_body,
        grid=(s // t,),
        in_specs=[nat(D_MODEL), _resident((1, D_MODEL)), _resident((D_MODEL, N_NAT)), _resident((N_TR, D_MODEL))],
        out_specs=[nat(1152), nat(256), nat(384), tr(256), tr(256), tr(384), tr(384),
                   pl.BlockSpec((1, SUBLANES, LANES), lambda i: (i, 0, 0))],
        out_shape=[jax.ShapeDtypeStruct((s, 1152), F32),
                   jax.ShapeDtypeStruct((s, 256), BF16), jax.ShapeDtypeStruct((s, 384), BF16),
                   jax.ShapeDtypeStruct((256, s), BF16), jax.ShapeDtypeStruct((256, s), BF16),
                   jax.ShapeDtypeStruct((384, s), BF16), jax.ShapeDtypeStruct((384, s), BF16),
                   jax.ShapeDtypeStruct((s // t, SUBLANES, LANES), F32)],
        scratch_shapes=[pltpu.VMEM((1, LANES), F32)],
        compiler_params=_params("arbitrary"),
        name="inproj",
    )(x, g, wn, wt)


def _dil_body(slope_ref, q_ref, kp_ref, kc_ref, vp_ref, vc_ref, o_ref, kbuf, vbuf, oacc, macc, lacc):
    n = pl.program_id(1)
    kbuf[0:DIL_CHUNK, :] = kp_ref[...]
    kbuf[DIL_CHUNK:, :] = kc_ref[...]
    vbuf[0:DIL_CHUNK, :] = vp_ref[...]
    vbuf[DIL_CHUNK:, :] = vc_ref[...]
    slope = slope_ref[0]
    lane = lax.broadcasted_iota(jnp.int32, (1, LANES), 1)
    head0 = lane < HEAD_DIM
    slope0 = jnp.max(jnp.where(head0, slope, 0.0), axis=1, keepdims=True)
    slope1 = jnp.max(jnp.where(head0, 0.0, slope), axis=1, keepdims=True)
    qi = lax.broadcasted_iota(jnp.int32, (N_BACK, 2 * N_BACK), 0)
    ki = lax.broadcasted_iota(jnp.int32, (N_BACK, 2 * N_BACK), 1)
    dist = N_BACK + qi - ki
    band = (dist >= 0) & (dist <= N_BACK)
    distf = dist.astype(F32)
    nt = (((1,), (1,)), ((), ()))

    for bi, (window, d) in enumerate(DILATED_PAIRS):
        per_chunk = DIL_CHUNK // window

        def tile(it, carry, d=d, window=window, per_chunk=per_chunk, bi=bi):
            if d == 1:
                c, r = it, 0
            elif per_chunk == 1:
                c, r = 0, it
            else:
                c, r = it // d, it % d
            q_start = c * window + r
            k_start = DIL_CHUNK + q_start - window
            if d == 1:
                rows_q = pl.ds(q_start, N_BACK)
                rows_k = pl.ds(k_start, 2 * N_BACK)
            else:
                rows_q = pl.ds(q_start, N_BACK, stride=d)
                rows_k = pl.ds(k_start, 2 * N_BACK, stride=d)
            q2 = q_ref[rows_q, :]
            k2 = kbuf[rows_k, :].astype(BF16)
            v2 = vbuf[rows_k, :].astype(BF16)
            first = jnp.logical_and(n == 0, c == 0)
            valid = band & jnp.logical_not(jnp.logical_and(first, ki < N_BACK))
            outs, ms, ls = [], [], []
            for e, (hm, sl) in enumerate(((head0, slope0), (jnp.logical_not(head0), slope1))):
                qe = jnp.where(hm, q2, 0.0).astype(BF16)
                sc = lax.dot_general(qe, k2, nt, preferred_element_type=F32)
                sc = jnp.where(valid, sc - (sl * float(d)) * distf, -jnp.inf)
                m = jnp.max(sc, axis=1, keepdims=True)
                p = jnp.exp(sc - m)
                ls.append(jnp.sum(p, axis=1, keepdims=True))
                ms.append(m)
                outs.append(jnp.dot(p.astype(BF16), v2, preferred_element_type=F32))
            o_blk = jnp.where(head0, outs[0], outs[1])
            m_blk = jnp.where(head0, ms[0], ms[1])
            l_blk = jnp.where(head0, ls[0], ls[1])
            if bi == 0:
                oacc[rows_q, :] = o_blk
                macc[rows_q, :] = m_blk
                lacc[rows_q, :] = l_blk
            else:
                m_old = macc[rows_q, :]
                m_new = jnp.maximum(m_old, m_blk)
                a = jnp.exp(m_old - m_new)
                b = jnp.exp(m_blk - m_new)
                oacc[rows_q, :] = a * oacc[rows_q, :] + b * o_blk
                lacc[rows_q, :] = a * lacc[rows_q, :] + b * l_blk
                macc[rows_q, :] = m_new
            return carry

        lax.fori_loop(0, DIL_CHUNK // N_BACK, tile, 0, unroll=DIL_UNROLL)

    o_ref[...] = (oacc[...] / lacc[...]).astype(BF16)


def _dilated(dil, slopes):
    s = dil.shape[0]
    npairs = N_HEADS_DIL // 2
    blk = lambda col0, prev: pl.BlockSpec(
        (DIL_CHUNK, LANES),
        (lambda p, n: (jnp.maximum(n - 1, 0), col0 + p)) if prev else (lambda p, n: (n, col0 + p)))
    buf = lambda rows: pltpu.VMEM((rows, LANES), F32)
    return pl.pallas_call(
        _dil_body,
        grid=(npairs, s // DIL_CHUNK),
        in_specs=[pl.BlockSpec((1, 1, LANES), lambda p, n: (p, 0, 0)),
                  blk(0, False), blk(3, True), blk(3, False), blk(6, True), blk(6, False)],
        out_specs=pl.BlockSpec((DIL_CHUNK, LANES), lambda p, n: (n, p)),
        out_shape=jax.ShapeDtypeStruct((s, N_HEADS_DIL * HEAD_DIM), BF16),
        scratch_shapes=[buf(2 * DIL_CHUNK), buf(2 * DIL_CHUNK), buf(DIL_CHUNK), buf(DIL_CHUNK), buf(DIL_CHUNK)],
        compiler_params=_params("parallel", "parallel"),
        name="dilated",
    )(slopes, dil, dil, dil, dil, dil)


def _diff_body(lam_init, kn_ref, slope_ref, lam_ref, gain_ref, qt_ref, k_ref, vt_ref, o_ref,
               m_ref, l_ref, acc_ref, s0_ref, s1_ref):
    tq, tkv = ATT_TQ, ATT_TKV
    pair = pl.program_id(0)
    i = pl.program_id(1)
    qt = qt_ref[...]
    row = lax.broadcasted_iota(jnp.int32, (LANES, tq), 0)
    zero = jnp.zeros_like(qt)
    top = jnp.concatenate(
        [jnp.where((row >= DIFF_QK_DIM * c) & (row < DIFF_QK_DIM * (c + 1)), qt, zero) for c in range(4)], axis=1)
    sl2 = slope_ref[0] * LOG2E
    sl_hi = sl2.astype(BF16).astype(F32)
    sl_lo = sl2 - sl_hi
    brow = lax.broadcasted_iota(jnp.int32, (LANES, 4 * tq), 0)
    bottom = jnp.where(brow == 0, sl_hi, jnp.where(brow == 1, sl_lo, 0.0)).astype(BF16)
    qa = jnp.concatenate([top, bottom], axis=0)
    prow = lax.broadcasted_iota(jnp.int32, (tkv, LANES), 0)
    pcol = lax.broadcasted_iota(jnp.int32, (tkv, LANES), 1)
    kofs = jnp.where(pcol < 2, prow, 0).astype(F32).astype(BF16)
    topf = top.astype(F32)
    qn = jnp.sqrt(jnp.sum(topf * topf, axis=0, keepdims=True))
    kpos = lax.broadcasted_iota(jnp.int32, (tkv, 4 * tq), 0)
    qpos = lax.broadcasted_iota(jnp.int32, (tkv, 4 * tq), 1) & (tq - 1)
    causal = kpos <= qpos

    m_ref[...] = jnp.full(m_ref.shape, NEG_BIG, F32)
    l_ref[...] = jnp.zeros(l_ref.shape, F32)
    acc_ref[...] = jnp.zeros(acc_ref.shape, F32)

    def tile_start(j):
        return pl.multiple_of(jnp.maximum(j, 0) * tkv, tkv)

    def scores(j):
        k = k_ref[pl.ds(tile_start(j), tkv), :]
        return jnp.dot(jnp.concatenate([k, kofs], axis=1), qa, preferred_element_type=F32)

    def softmax_step(s, j, masked):
        if masked:
            s = jnp.where(causal, s, -jnp.inf)
        c = jnp.where(j >= 0, sl2 * (j * tkv - i * tq).astype(F32), NEG_BIG)
        m_old = m_ref[...]
        m_new = jnp.maximum(m_old, jnp.max(s, axis=0, keepdims=True) + c)
        p = jnp.exp2(s - (m_new - c))
        alpha = jnp.exp2(m_old - m_new)
        l_ref[...] = alpha * l_ref[...] + jnp.sum(p, axis=0, keepdims=True)
        m_ref[...] = m_new
        pb = p.astype(BF16)
        vt = vt_ref[:, pl.ds(tile_start(j), tkv)]
        h = HEAD_DIM
        acc_ref[0:h, :] = alpha[:, 0:2 * tq] * acc_ref[0:h, :] + jnp.dot(
            vt[0:h], pb[:, 0:2 * tq], preferred_element_type=F32)
        acc_ref[h:, :] = alpha[:, 2 * tq:] * acc_ref[h:, :] + jnp.dot(
            vt[h:], pb[:, 2 * tq:], preferred_element_type=F32)

    def alive(j):
        kn = kn_ref[pair, jnp.maximum(j, 0)]
        reach = qn * kn + sl2 * ((j * tkv - i * tq).astype(F32) + (tkv - 1.0)) - m_ref[...]
        return jnp.max(reach) > DEAD_EXP2

    s_diag = scores(i)
    s0_ref[...] = scores(i - 1)
    softmax_step(s_diag, i, True)

    def more(state):
        u, live = state
        return jnp.logical_and(2 * u < i, live)

    def two_tiles(state):
        u, _ = state
        ja = i - 1 - 2 * u
        live = alive(ja - 2)
        s1_ref[...] = scores(ja - 1)
        softmax_step(s0_ref[...], ja, False)
        s0_ref[...] = scores(ja - 2)
        softmax_step(s1_ref[...], ja - 1, False)
        return u + 1, live

    lax.while_loop(more, two_tiles, (0, alive(i - 1)))

    lp = lam_ref[...]
    lam = (jnp.exp(jnp.sum(lp[0:1] * lp[1:2], axis=1, keepdims=True))
           - jnp.exp(jnp.sum(lp[2:3] * lp[3:4], axis=1, keepdims=True)) + lam_init)
    inv = 1.0 / l_ref[...]
    acc = acc_ref[...]
    outs = []
    for e in range(2):
        a = acc[e * HEAD_DIM:(e + 1) * HEAD_DIM]
        i1 = inv[:, (2 * e) * tq:(2 * e + 1) * tq]
        i2 = inv[:, (2 * e + 1) * tq:(2 * e + 2) * tq]
        o = a[:, 0:tq] * i1 - lam * (a[:, tq:] * i2)
        o = o * lax.rsqrt(jnp.mean(o * o, axis=0, keepdims=True) + NORM_EPS)
        outs.append(o * gain_ref[...] * (1.0 - lam_init))
    o_ref[...] = jnp.concatenate(outs, axis=0).T.astype(BF16)


def _diff_attention(qbt, kb, vbt, knorm, lam_params, gain_col, slopes, lam_init):
    s = kb.shape[0]
    tq = ATT_TQ
    npairs = N_HEADS_DIFF // 2
    return pl.pallas_call(
        functools.partial(_diff_body, lam_init),
        grid=(npairs, s // tq),
        in_specs=[pl.BlockSpec(memory_space=pltpu.SMEM),
                  pl.BlockSpec((1, 1, 4 * tq), lambda p, i: (p, 0, 0)),
                  pl.BlockSpec((4, DIFF_QK_DIM), lambda p, i: (0, 0)),
                  pl.BlockSpec((HEAD_DIM, 1), lambda p, i: (0, 0)),
                  pl.BlockSpec((LANES, tq), lambda p, i: (p, i)),
                  pl.BlockSpec((s, LANES), lambda p, i: (0, p)),
                  pl.BlockSpec((LANES, s), lambda p, i: (p, 0))],
        out_specs=pl.BlockSpec((tq, LANES), lambda p, i: (i, p)),
        out_shape=jax.ShapeDtypeStruct((s, N_HEADS_DIFF * HEAD_DIM), BF16),
        scratch_shapes=[pltpu.VMEM((1, 4 * tq), F32), pltpu.VMEM((1, 4 * tq), F32),
                        pltpu.VMEM((LANES, 2 * tq), F32),
                        pltpu.VMEM((ATT_TKV, 4 * tq), F32), pltpu.VMEM((ATT_TKV, 4 * tq), F32)],
        compiler_params=_params("parallel", "arbitrary"),
        name="diff_attn",
    )(knorm, slopes, lam_params, gain_col, qbt, kb, vbt)


def _sb_body(qt_ref, k_ref, vt_ref, o_ref, carry_ref, acc_ref):
    tq, tkv = ATT_TQ, ATT_TKV
    i = pl.program_id(1)
    qt = qt_ref[...]
    row = lax.broadcasted_iota(jnp.int32, (LANES, tq), 0)
    zero = jnp.zeros_like(qt)
    qa = jnp.concatenate([jnp.where(row < HEAD_DIM, qt, zero), jnp.where(row >= HEAD_DIM, qt, zero)], axis=1)
    ur = lax.broadcasted_iota(jnp.int32, (tkv, tkv), 0)
    uc = lax.broadcasted_iota(jnp.int32, (tkv, tkv), 1)
    upper = (uc >= ur).astype(BF16)
    kpos = lax.broadcasted_iota(jnp.int32, (tkv, 2 * tq), 0)
    qpos = lax.broadcasted_iota(jnp.int32, (tkv, 2 * tq), 1) & (tq - 1)
    causal = kpos < qpos

    carry_ref[...] = jnp.zeros(carry_ref.shape, F32)
    acc_ref[...] = jnp.zeros(acc_ref.shape, F32)

    def tile(j, masked):
        ks = pl.multiple_of(j * tkv, tkv)
        k = k_ref[pl.ds(ks, tkv), :]
        z = jnp.dot(k, qa, preferred_element_type=F32)
        sp = jnp.maximum(z, 0.0) + jnp.log(1.0 + jnp.exp2(-jnp.abs(z))) * LOG2E
        if masked:
            sp = jnp.where(causal, sp, 0.0)
        hi = sp.astype(BF16)
        lo = (sp - hi.astype(F32)).astype(BF16)
        cs = (jnp.dot(upper, hi, preferred_element_type=F32)
              + jnp.dot(upper, lo, preferred_element_type=F32))
        w = jnp.exp2((z + carry_ref[...]) - cs)
        if masked:
            w = jnp.where(causal, w, 0.0)
        wb = w.astype(BF16)
        carry_ref[...] = carry_ref[...] - cs[0:1]
        vt = vt_ref[:, pl.ds(ks, tkv)]
        h = HEAD_DIM
        acc_ref[0:h, :] = acc_ref[0:h, :] + jnp.dot(vt[0:h], wb[:, 0:tq], preferred_element_type=F32)
        acc_ref[h:, :] = acc_ref[h:, :] + jnp.dot(vt[h:], wb[:, tq:], preferred_element_type=F32)

    def alive():
        return jnp.max(carry_ref[...]) > DEAD_EXP2

    tile(i, True)

    def more(state):
        it, live = state
        return jnp.logical_and(it < i, live)

    def full_tile(state):
        it, _ = state
        tile(i - 1 - it, False)
        return it + 1, alive()

    lax.while_loop(more, full_tile, (0, alive()))
    o_ref[...] = acc_ref[...].T.astype(BF16)


def _sb_attention(qct, kc, vct):
    s = kc.shape[0]
    tq = ATT_TQ
    npairs = N_HEADS_SB // 2
    return pl.pallas_call(
        _sb_body,
        grid=(npairs, s // tq),
        in_specs=[pl.BlockSpec((LANES, tq), lambda p, i: (p, i)),
                  pl.BlockSpec((s, LANES), lambda p, i: (0, p)),
                  pl.BlockSpec((LANES, s), lambda p, i: (p, 0))],
        out_specs=pl.BlockSpec((tq, LANES), lambda p, i: (i, p)),
        out_shape=jax.ShapeDtypeStruct((s, N_HEADS_SB * HEAD_DIM), BF16),
        scratch_shapes=[pltpu.VMEM((1, 2 * tq), F32), pltpu.VMEM((LANES, tq), F32)],
        compiler_params=_params("parallel", "arbitrary"),
        name="sb_attn",
    )(qct, kc, vct)


def _outproj_body(x_ref, g_ref, oa_ref, ob_ref, oc_ref, wa_ref, wb_ref, wc_ref, o_ref):
    y = (jnp.dot(oa_ref[...], wa_ref[...], preferred_element_type=F32)
         + jnp.dot(ob_ref[...], wb_ref[...], preferred_element_type=F32)
         + jnp.dot(oc_ref[...], wc_ref[...], preferred_element_type=F32))
    o_ref[...] = x_ref[...] + _rms(y, g_ref[...])


def _outproj(x, g, oa, ob, oc, wa, wb, wc):
    s = x.shape[0]
    t = PROJ_ROWS
    nat = lambda w: pl.BlockSpec((t, w), lambda i: (i, 0))
    return pl.pallas_call(
        _outproj_body,
        grid=(s // t,),
        in_specs=[nat(D_MODEL), _resident((1, D_MODEL)), nat(384), nat(256), nat(384),
                  _resident((384, D_MODEL)), _resident((256, D_MODEL)), _resident((384, D_MODEL))],
        out_specs=nat(D_MODEL),
        out_shape=jax.ShapeDtypeStruct((s, D_MODEL), F32),
        compiler_params=_params("parallel"),
        name="outproj",
    )(x, g, oa, ob, oc, wa, wb, wc)


def _alibi_slopes(n):
    return 2.0 ** (-8.0 * jnp.arange(1, n + 1, dtype=F32) / n)


def kernel(x, norm_gains, w_ffn_gate, w_ffn_up, w_ffn_down, w_in, w_out, diff_lambda, diff_subln_gain):
    b, s, d = x.shape
    assert b == 1 and d == D_MODEL and s % DIL_CHUNK == 0 and ATT_TQ == ATT_TKV
    depth = norm_gains.shape[0]
    xs = x.reshape(s, d)

    off = [0]
    for w in IN_SIZES:
        off.append(off[-1] + w)
    col = lambda a, i: a[:, off[i]:off[i + 1]]

    sl_dil = jnp.repeat(_alibi_slopes(N_HEADS_DIL), HEAD_DIM).reshape(N_HEADS_DIL // 2, 1, LANES)
    sl_diff = jnp.repeat(_alibi_slopes(N_HEADS_DIFF), 2 * ATT_TQ).reshape(N_HEADS_DIFF // 2, 1, 4 * ATT_TQ)

    for layer in range(depth):
        g = norm_gains[layer].reshape(-1, 1, D_MODEL)
        wg = w_ffn_gate[layer].astype(BF16)
        wu = w_ffn_up[layer].astype(BF16)
        wd = w_ffn_down[layer].astype(BF16)
        wi = w_in[layer]
        wn = jnp.concatenate([col(wi, 0), col(wi, 1), col(wi, 2), col(wi, 4), col(wi, 7)], axis=1).astype(BF16)
        wt = jnp.concatenate([col(wi, 3), col(wi, 5), col(wi, 6), col(wi, 8)], axis=1).T.astype(BF16)
        wo = w_out[layer].astype(BF16)
        lam_init = 0.8 - 0.6 * math.exp(-0.3 * layer)

        xs = _ffn(xs, g[0], g[1], wg[0], wu[0], wd[0])
        dil, kb, kc, qbt, vbt, qct, vct, kn = _inproj(xs, g[2], wn, wt)
        tiles_per_step = PROJ_ROWS // ATT_TKV
        knorm = kn[:, :tiles_per_step, :N_HEADS_DIFF // 2].reshape(s // ATT_TKV, N_HEADS_DIFF // 2).T
        oa = _dilated(dil, sl_dil)
        ob = _diff_attention(qbt, kb, vbt, knorm, diff_lambda[layer], diff_subln_gain[layer].reshape(HEAD_DIM, 1),
                             sl_diff, lam_init)
        oc = _sb_attention(qct, kc, vct)
        xs = _outproj(xs, g[3], oa, ob, oc, wo[0:384], wo[384:640], wo[640:1024])
        xs = _ffn(xs, g[4], g[5], wg[1], wu[1], wd[1])
    return xs.reshape(b, s, d)
```

```python
import functools
import math

import jax
import jax.numpy as jnp
from jax import lax
from jax.experimental import pallas as pl
from jax.experimental.pallas import tpu as pltpu

D_MODEL = 1024
D_FF = 2816
HEAD_DIM = 64
N_HEADS_DIL = 6
N_HEADS_DIFF = 4
N_HEADS_SB = 6
DIFF_QK_DIM = HEAD_DIM // 2
DILATED_PAIRS = ((128, 1), (512, 4), (2048, 16))
N_BACK = 128
NORM_EPS = 1e-6
IN_SIZES = (384, 384, 384, 256, 256, 256, 384, 384, 384)

LANES = 128
SUBLANES = 8
VMEM_LIMIT = 56 * 1024 * 1024

FFN_ROWS = 512
FFN_CHUNKS = 11
PROJ_ROWS = 512
DIL_CHUNK = 2048
DIL_UNROLL = 8
ATT_TQ = 256
ATT_TKV = 256
NEG_BIG = -1e30
LOG2E = 1.4426950408889634
DEAD_EXP2 = -160.0

F32 = jnp.float32
BF16 = jnp.bfloat16


def _rms(x, g):
    return x * lax.rsqrt(jnp.mean(x * x, axis=-1, keepdims=True) + NORM_EPS) * g


def _params(*sem):
    return pltpu.CompilerParams(dimension_semantics=sem, vmem_limit_bytes=VMEM_LIMIT)


def _resident(shape):
    zeros = (0,) * len(shape)
    return pl.BlockSpec(shape, lambda *_: zeros, pipeline_mode=pl.Buffered(1))


def _picked(shape, *lead):
    idx = tuple(lead) + (0,) * len(shape)
    return pl.BlockSpec((None,) * len(lead) + tuple(shape), lambda *_: idx, pipeline_mode=pl.Buffered(1))


def _ffn_body(x_ref, gpre_ref, gpost_ref, wg_ref, wu_ref, wd_ref, o_ref):
    x = x_ref[...]
    hb = _rms(x, gpre_ref[...]).astype(BF16)
    step = D_FF // FFN_CHUNKS
    y = None
    for c in range(FFN_CHUNKS):
        sl = slice(c * step, (c + 1) * step)
        g = jnp.dot(hb, wg_ref[:, sl], preferred_element_type=F32)
        u = jnp.dot(hb, wu_ref[:, sl], preferred_element_type=F32)
        a = (g / (1.0 + jnp.exp(-g)) * u).astype(BF16)
        part = jnp.dot(a, wd_ref[sl, :], preferred_element_type=F32)
        y = part if y is None else y + part
    o_ref[...] = x + 0.5 * _rms(y, gpost_ref[...])


def _ffn(x, gains, wg, wu, wd, layer, half):
    s = x.shape[0]
    row = pl.BlockSpec((FFN_ROWS, D_MODEL), lambda i: (i, 0))
    return pl.pallas_call(
        _ffn_body,
        grid=(s // FFN_ROWS,),
        in_specs=[row, _picked((1, D_MODEL), layer, 4 * half), _picked((1, D_MODEL), layer, 4 * half + 1),
                  _picked((D_MODEL, D_FF), layer, half), _picked((D_MODEL, D_FF), layer, half),
                  _picked((D_FF, D_MODEL), layer, half)],
        out_specs=row,
        out_shape=jax.ShapeDtypeStruct((s, D_MODEL), F32),
        compiler_params=_params("parallel"),
        name="ffn",
    )(x, gains, gains, wg, wu, wd)


N_NAT = 384 * 3 + 256 + 384
N_TR = 256 + 256 + 384 + 384


def _inproj_body(x_ref, g_ref, wn_ref, wt_ref, dil_ref, kb_ref, kc_ref, qbt_ref, vbt_ref, qct_ref, vct_ref,
                 kn_ref, run_ref):
    hb = _rms(x_ref[...], g_ref[...]).astype(BF16)
    pn = jnp.dot(hb, wn_ref[...], preferred_element_type=F32)
    dil_ref[:, 0:384] = pn[:, 0:384] * (HEAD_DIM ** -0.5)
    dil_ref[:, 384:1152] = pn[:, 384:1152]
    kbf = pn[:, 1152:1408].astype(BF16)
    kb_ref[...] = kbf
    kc_ref[...] = pn[:, 1408:1792].astype(BF16)
    pt = lax.dot_general(wt_ref[...], hb, (((1,), (1,)), ((), ())), preferred_element_type=F32)
    qbt_ref[...] = (pt[0:256] * (DIFF_QK_DIM ** -0.5 * LOG2E)).astype(BF16)
    vbt_ref[...] = pt[256:512].astype(BF16)
    qct_ref[...] = (pt[512:896] * (HEAD_DIM ** -0.5 * LOG2E)).astype(BF16)
    vct_ref[...] = pt[896:1280].astype(BF16)

    @pl.when(pl.program_id(0) == 0)
    def _():
        run_ref[...] = jnp.zeros(run_ref.shape, F32)

    k2 = kbf.astype(F32)
    k2 = k2 * k2
    n2 = [jnp.sum(k2[:, p * LANES:(p + 1) * LANES], axis=1, keepdims=True) for p in range(N_HEADS_DIFF // 2)]
    lane = lax.broadcasted_iota(jnp.int32, (1, LANES), 1)
    rowi = lax.broadcasted_iota(jnp.int32, (SUBLANES, LANES), 0)
    run = run_ref[...]
    out = jnp.zeros((SUBLANES, LANES), F32)
    for r in range(PROJ_ROWS // ATT_TKV):
        tmax = [jnp.max(n[r * ATT_TKV:(r + 1) * ATT_TKV], axis=0, keepdims=True) for n in n2]
        run = jnp.maximum(run, jnp.where(lane == 0, tmax[0], jnp.where(lane == 1, tmax[1], 0.0)))
        out = jnp.where(rowi == r, jnp.sqrt(run), out)
    run_ref[...] = run
    kn_ref[0] = out


def _inproj(x, gains, wn, wt, layer):
    s = x.shape[0]
    t = PROJ_ROWS
    nat = lambda w: pl.BlockSpec((t, w), lambda i: (i, 0))
    tr = lambda w: pl.BlockSpec((w, t), lambda i: (0, i))
    return pl.pallas_call(
        _inproj_body,
        grid=(s // t,),
        in_specs=[nat(D_MODEL), _picked((1, D_MODEL), layer, 2), _picked((D_MODEL, N_NAT), layer),
                  _picked((N_TR, D_MODEL), layer)],
        out_specs=[nat(1152), nat(256), nat(384), tr(256), tr(256), tr(384), tr(384),
                   pl.BlockSpec((1, SUBLANES, LANES), lambda i: (i, 0, 0))],
        out_shape=[jax.ShapeDtypeStruct((s, 1152), F32),
                   jax.ShapeDtypeStruct((s, 256), BF16), jax.ShapeDtypeStruct((s, 384), BF16),
                   jax.ShapeDtypeStruct((256, s), BF16), jax.ShapeDtypeStruct((256, s), BF16),
                   jax.ShapeDtypeStruct((384, s), BF16), jax.ShapeDtypeStruct((384, s), BF16),
                   jax.ShapeDtypeStruct((s // t, SUBLANES, LANES), F32)],
        scratch_shapes=[pltpu.VMEM((1, LANES), F32)],
        compiler_params=_params("arbitrary"),
        name="inproj",
    )(x, gains, wn, wt)


def _dil_body(slope_ref, q_ref, kp_ref, kc_ref, vp_ref, vc_ref, o_ref, kbuf, vbuf, oacc, macc, lacc):
    n = pl.program_id(1)
    kbuf[0:DIL_CHUNK, :] = kp_ref[...]
    kbuf[DIL_CHUNK:, :] = kc_ref[...]
    vbuf[0:DIL_CHUNK, :] = vp_ref[...]
    vbuf[DIL_CHUNK:, :] = vc_ref[...]
    slope = slope_ref[0]
    lane = lax.broadcasted_iota(jnp.int32, (1, LANES), 1)
    head0 = lane < HEAD_DIM
    slope0 = jnp.max(jnp.where(head0, slope, 0.0), axis=1, keepdims=True)
    slope1 = jnp.max(jnp.where(head0, 0.0, slope), axis=1, keepdims=True)
    qi = lax.broadcasted_iota(jnp.int32, (N_BACK, 2 * N_BACK), 0)
    ki = lax.broadcasted_iota(jnp.int32, (N_BACK, 2 * N_BACK), 1)
    dist = N_BACK + qi - ki
    band = (dist >= 0) & (dist <= N_BACK)
    distf = dist.astype(F32)
    nt = (((1,), (1,)), ((), ()))

    for bi, (window, d) in enumerate(DILATED_PAIRS):
        per_chunk = DIL_CHUNK // window

        def tile(it, carry, d=d, window=window, per_chunk=per_chunk, bi=bi):
            if d == 1:
                c, r = it, 0
            elif per_chunk == 1:
                c, r = 0, it
            else:
                c, r = it // d, it % d
            q_start = c * window + r
            k_start = DIL_CHUNK + q_start - window
            if d == 1:
                rows_q = pl.ds(q_start, N_BACK)
                rows_k = pl.ds(k_start, 2 * N_BACK)
            else:
                rows_q = pl.ds(q_start, N_BACK, stride=d)
                rows_k = pl.ds(k_start, 2 * N_BACK, stride=d)
            q2 = q_ref[rows_q, :]
            k2 = kbuf[rows_k, :].astype(BF16)
            v2 = vbuf[rows_k, :].astype(BF16)
            first = jnp.logical_and(n == 0, c == 0)
            valid = band & jnp.logical_not(jnp.logical_and(first, ki < N_BACK))
            outs, ms, ls = [], [], []
            for e, (hm, sl) in enumerate(((head0, slope0), (jnp.logical_not(head0), slope1))):
                qe = jnp.where(hm, q2, 0.0).astype(BF16)
                sc = lax.dot_general(qe, k2, nt, preferred_element_type=F32)
                sc = jnp.where(valid, sc - (sl * float(d)) * distf, -jnp.inf)
                m = jnp.max(sc, axis=1, keepdims=True)
                p = jnp.exp(sc - m)
                ls.append(jnp.sum(p, axis=1, keepdims=True))
                ms.append(m)
                outs.append(jnp.dot(p.astype(BF16), v2, preferred_element_type=F32))
            o_blk = jnp.where(head0, outs[0], outs[1])
            m_blk = jnp.where(head0, ms[0], ms[1])
            l_blk = jnp.where(head0, ls[0], ls[1])
            if bi == 0:
                oacc[rows_q, :] = o_blk
                macc[rows_q, :] = m_blk
                lacc[rows_q, :] = l_blk
            else:
                m_old = macc[rows_q, :]
                m_new = jnp.maximum(m_old, m_blk)
                a = jnp.exp(m_old - m_new)
                b = jnp.exp(m_blk - m_new)
                oacc[rows_q, :] = a * oacc[rows_q, :] + b * o_blk
                lacc[rows_q, :] = a * lacc[rows_q, :] + b * l_blk
                macc[rows_q, :] = m_new
            return carry

        lax.fori_loop(0, DIL_CHUNK // N_BACK, tile, 0, unroll=DIL_UNROLL)

    o_ref[...] = (oacc[...] / lacc[...]).astype(BF16)


def _dilated(dil, slopes):
    s = dil.shape[0]
    npairs = N_HEADS_DIL // 2
    blk = lambda col0, prev: pl.BlockSpec(
        (DIL_CHUNK, LANES),
        (lambda p, n: (jnp.maximum(n - 1, 0), col0 + p)) if prev else (lambda p, n: (n, col0 + p)))
    buf = lambda rows: pltpu.VMEM((rows, LANES), F32)
    return pl.pallas_call(
        _dil_body,
        grid=(npairs, s // DIL_CHUNK),
        in_specs=[pl.BlockSpec((1, 1, LANES), lambda p, n: (p, 0, 0)),
                  blk(0, False), blk(3, True), blk(3, False), blk(6, True), blk(6, False)],
        out_specs=pl.BlockSpec((DIL_CHUNK, LANES), lambda p, n: (n, p)),
        out_shape=jax.ShapeDtypeStruct((s, N_HEADS_DIL * HEAD_DIM), BF16),
        scratch_shapes=[buf(2 * DIL_CHUNK), buf(2 * DIL_CHUNK), buf(DIL_CHUNK), buf(DIL_CHUNK), buf(DIL_CHUNK)],
        compiler_params=_params("parallel", "parallel"),
        name="dilated",
    )(slopes, dil, dil, dil, dil, dil)


def _diff_body(lam_init, kn_ref, slope_ref, lam_ref, gain_ref, qt_ref, k_ref, vt_ref, o_ref,
               m_ref, l_ref, acc_ref, s0_ref, s1_ref):
    tq, tkv = ATT_TQ, ATT_TKV
    pair = pl.program_id(0)
    i = pl.program_id(1)
    qt = qt_ref[...]
    row = lax.broadcasted_iota(jnp.int32, (LANES, tq), 0)
    zero = jnp.zeros_like(qt)
    top = jnp.concatenate(
        [jnp.where((row >= DIFF_QK_DIM * c) & (row < DIFF_QK_DIM * (c + 1)), qt, zero) for c in range(4)], axis=1)
    sl2 = slope_ref[0] * LOG2E
    sl_hi = sl2.astype(BF16).astype(F32)
    sl_lo = sl2 - sl_hi
    brow = lax.broadcasted_iota(jnp.int32, (LANES, 4 * tq), 0)
    bottom = jnp.where(brow == 0, sl_hi, jnp.where(brow == 1, sl_lo, 0.0)).astype(BF16)
    qa = jnp.concatenate([top, bottom], axis=0)
    prow = lax.broadcasted_iota(jnp.int32, (tkv, LANES), 0)
    pcol = lax.broadcasted_iota(jnp.int32, (tkv, LANES), 1)
    kofs = jnp.where(pcol < 2, prow, 0).astype(F32).astype(BF16)
    topf = top.astype(F32)
    qn = jnp.sqrt(jnp.sum(topf * topf, axis=0, keepdims=True))
    kpos = lax.broadcasted_iota(jnp.int32, (tkv, 4 * tq), 0)
    qpos = lax.broadcasted_iota(jnp.int32, (tkv, 4 * tq), 1) & (tq - 1)
    causal = kpos <= qpos

    m_ref[...] = jnp.full(m_ref.shape, NEG_BIG, F32)
    l_ref[...] = jnp.zeros(l_ref.shape, F32)
    acc_ref[...] = jnp.zeros(acc_ref.shape, F32)

    def tile_start(j):
        return pl.multiple_of(jnp.maximum(j, 0) * tkv, tkv)

    def scores(j):
        k = k_ref[pl.ds(tile_start(j), tkv), :]
        return jnp.dot(jnp.concatenate([k, kofs], axis=1), qa, preferred_element_type=F32)

    def softmax_step(s, j, masked):
        if masked:
            s = jnp.where(causal, s, -jnp.inf)
        c = jnp.where(j >= 0, sl2 * (j * tkv - i * tq).astype(F32), NEG_BIG)
        m_old = m_ref[...]
        m_new = jnp.maximum(m_old, jnp.max(s, axis=0, keepdims=True) + c)
        p = jnp.exp2(s - (m_new - c))
        alpha = jnp.exp2(m_old - m_new)
        l_ref[...] = alpha * l_ref[...] + jnp.sum(p, axis=0, keepdims=True)
        m_ref[...] = m_new
        pb = p.astype(BF16)
        vt = vt_ref[:, pl.ds(tile_start(j), tkv)]
        h = HEAD_DIM
        acc_ref[0:h, :] = alpha[:, 0:2 * tq] * acc_ref[0:h, :] + jnp.dot(
            vt[0:h], pb[:, 0:2 * tq], preferred_element_type=F32)
        acc_ref[h:, :] = alpha[:, 2 * tq:] * acc_ref[h:, :] + jnp.dot(
            vt[h:], pb[:, 2 * tq:], preferred_element_type=F32)

    def alive(j):
        kn = kn_ref[pair, jnp.maximum(j, 0)]
        reach = qn * kn + sl2 * ((j * tkv - i * tq).astype(F32) + (tkv - 1.0)) - m_ref[...]
        return jnp.max(reach) > DEAD_EXP2

    s_diag = scores(i)
    s0_ref[...] = scores(i - 1)
    softmax_step(s_diag, i, True)

    def more(state):
        u, live = state
        return jnp.logical_and(2 * u < i, live)

    def two_tiles(state):
        u, _ = state
        ja = i - 1 - 2 * u
        live = alive(ja - 2)
        s1_ref[...] = scores(ja - 1)
        softmax_step(s0_ref[...], ja, False)
        s0_ref[...] = scores(ja - 2)
        softmax_step(s1_ref[...], ja - 1, False)
        return u + 1, live

    lax.while_loop(more, two_tiles, (0, alive(i - 1)))

    lp = lam_ref[...]
    lam = (jnp.exp(jnp.sum(lp[0:1] * lp[1:2], axis=1, keepdims=True))
           - jnp.exp(jnp.sum(lp[2:3] * lp[3:4], axis=1, keepdims=True)) + lam_init)
    inv = 1.0 / l_ref[...]
    acc = acc_ref[...]
    outs = []
    for e in range(2):
        a = acc[e * HEAD_DIM:(e + 1) * HEAD_DIM]
        i1 = inv[:, (2 * e) * tq:(2 * e + 1) * tq]
        i2 = inv[:, (2 * e + 1) * tq:(2 * e + 2) * tq]
        o = a[:, 0:tq] * i1 - lam * (a[:, tq:] * i2)
        o = o * lax.rsqrt(jnp.mean(o * o, axis=0, keepdims=True) + NORM_EPS)
        outs.append(o * gain_ref[...] * (1.0 - lam_init))
    o_ref[...] = jnp.concatenate(outs, axis=0).T.astype(BF16)


def _diff_attention(qbt, kb, vbt, knorm, lam_params, gain_col, slopes, layer, lam_init):
    s = kb.shape[0]
    tq = ATT_TQ
    npairs = N_HEADS_DIFF // 2
    return pl.pallas_call(
        functools.partial(_diff_body, lam_init),
        grid=(npairs, s // tq),
        in_specs=[pl.BlockSpec(memory_space=pltpu.SMEM),
                  pl.BlockSpec((1, 1, 4 * tq), lambda p, i: (p, 0, 0)),
                  pl.BlockSpec((None, 4, DIFF_QK_DIM), lambda p, i: (layer, 0, 0)),
                  pl.BlockSpec((None, HEAD_DIM, 1), lambda p, i: (layer, 0, 0)),
                  pl.BlockSpec((LANES, tq), lambda p, i: (p, i)),
                  pl.BlockSpec((s, LANES), lambda p, i: (0, p)),
                  pl.BlockSpec((LANES, s), lambda p, i: (p, 0))],
        out_specs=pl.BlockSpec((tq, LANES), lambda p, i: (i, p)),
        out_shape=jax.ShapeDtypeStruct((s, N_HEADS_DIFF * HEAD_DIM), BF16),
        scratch_shapes=[pltpu.VMEM((1, 4 * tq), F32), pltpu.VMEM((1, 4 * tq), F32),
                        pltpu.VMEM((LANES, 2 * tq), F32),
                        pltpu.VMEM((ATT_TKV, 4 * tq), F32), pltpu.VMEM((ATT_TKV, 4 * tq), F32)],
        compiler_params=_params("parallel", "arbitrary"),
        name="diff_attn",
    )(knorm, slopes, lam_params, gain_col, qbt, kb, vbt)


def _sb_body(qt_ref, k_ref, vt_ref, o_ref, carry_ref, acc_ref):
    tq, tkv = ATT_TQ, ATT_TKV
    i = pl.program_id(1)
    qt = qt_ref[...]
    row = lax.broadcasted_iota(jnp.int32, (LANES, tq), 0)
    zero = jnp.zeros_like(qt)
    qa = jnp.concatenate([jnp.where(row < HEAD_DIM, qt, zero), jnp.where(row >= HEAD_DIM, qt, zero)], axis=1)
    ur = lax.broadcasted_iota(jnp.int32, (tkv, tkv), 0)
    uc = lax.broadcasted_iota(jnp.int32, (tkv, tkv), 1)
    upper = (uc >= ur).astype(BF16)
    kpos = lax.broadcasted_iota(jnp.int32, (tkv, 2 * tq), 0)
    qpos = lax.broadcasted_iota(jnp.int32, (tkv, 2 * tq), 1) & (tq - 1)
    causal = kpos < qpos

    carry_ref[...] = jnp.zeros(carry_ref.shape, F32)
    acc_ref[...] = jnp.zeros(acc_ref.shape, F32)

    def tile(j, masked, guarded=False):
        ks = pl.multiple_of(jnp.maximum(j, 0) * tkv, tkv)
        k = k_ref[pl.ds(ks, tkv), :]
        z = jnp.dot(k, qa, preferred_element_type=F32)
        if guarded:
            z = jnp.where(j >= 0, z, NEG_BIG)
        sp = jnp.maximum(z, 0.0) + jnp.log(1.0 + jnp.exp2(-jnp.abs(z))) * LOG2E
        if masked:
            sp = jnp.where(causal, sp, 0.0)
        hi = sp.astype(BF16)
        lo = (sp - hi.astype(F32)).astype(BF16)
        cs = (jnp.dot(upper, hi, preferred_element_type=F32)
              + jnp.dot(upper, lo, preferred_element_type=F32))
        w = jnp.exp2((z + carry_ref[...]) - cs)
        if masked:
            w = jnp.where(causal, w, 0.0)
        wb = w.astype(BF16)
        carry_ref[...] = carry_ref[...] - cs[0:1]
        vt = vt_ref[:, pl.ds(ks, tkv)]
        h = HEAD_DIM
        acc_ref[0:h, :] = acc_ref[0:h, :] + jnp.dot(vt[0:h], wb[:, 0:tq], preferred_element_type=F32)
        acc_ref[h:, :] = acc_ref[h:, :] + jnp.dot(vt[h:], wb[:, tq:], preferred_element_type=F32)

    def alive():
        return jnp.max(carry_ref[...]) > DEAD_EXP2

    tile(i, True)
    tile(i - 1, False, guarded=True)

    def more(state):
        it, live = state
        return jnp.logical_and(it < i, live)

    def full_tile(state):
        it, _ = state
        tile(i - 1 - it, False)
        return it + 1, alive()

    lax.while_loop(more, full_tile, (1, alive()))
    o_ref[...] = acc_ref[...].T.astype(BF16)


def _sb_attention(qct, kc, vct):
    s = kc.shape[0]
    tq = ATT_TQ
    npairs = N_HEADS_SB // 2
    return pl.pallas_call(
        _sb_body,
        grid=(npairs, s // tq),
        in_specs=[pl.BlockSpec((LANES, tq), lambda p, i: (p, i)),
                  pl.BlockSpec((s, LANES), lambda p, i: (0, p)),
                  pl.BlockSpec((LANES, s), lambda p, i: (p, 0))],
        out_specs=pl.BlockSpec((tq, LANES), lambda p, i: (i, p)),
        out_shape=jax.ShapeDtypeStruct((s, N_HEADS_SB * HEAD_DIM), BF16),
        scratch_shapes=[pltpu.VMEM((1, 2 * tq), F32), pltpu.VMEM((LANES, tq), F32)],
        compiler_params=_params("parallel", "arbitrary"),
        name="sb_attn",
    )(qct, kc, vct)


def _outproj_body(x_ref, g_ref, oa_ref, ob_ref, oc_ref, w_ref, o_ref):
    a, b = N_HEADS_DIL * HEAD_DIM, (N_HEADS_DIL + N_HEADS_DIFF) * HEAD_DIM
    y = (jnp.dot(oa_ref[...], w_ref[0:a, :], preferred_element_type=F32)
         + jnp.dot(ob_ref[...], w_ref[a:b, :], preferred_element_type=F32)
         + jnp.dot(oc_ref[...], w_ref[b:, :], preferred_element_type=F32))
    o_ref[...] = x_ref[...] + _rms(y, g_ref[...])


def _outproj(x, gains, oa, ob, oc, wo, layer):
    s = x.shape[0]
    t = PROJ_ROWS
    nat = lambda w: pl.BlockSpec((t, w), lambda i: (i, 0))
    return pl.pallas_call(
        _outproj_body,
        grid=(s // t,),
        in_specs=[nat(D_MODEL), _picked((1, D_MODEL), layer, 3), nat(384), nat(256), nat(384),
                  _picked((D_MODEL, D_MODEL), layer)],
        out_specs=nat(D_MODEL),
        out_shape=jax.ShapeDtypeStruct((s, D_MODEL), F32),
        compiler_params=_params("parallel"),
        name="outproj",
    )(x, gains, oa, ob, oc, wo)


def _alibi_slopes(n):
    return 2.0 ** (-8.0 * jnp.arange(1, n + 1, dtype=F32) / n)


def kernel(x, norm_gains, w_ffn_gate, w_ffn_up, w_ffn_down, w_in, w_out, diff_lambda, diff_subln_gain):
    b, s, d = x.shape
    assert b == 1 and d == D_MODEL and s % DIL_CHUNK == 0 and ATT_TQ == ATT_TKV
    depth = norm_gains.shape[0]
    xs = x.reshape(s, d)

    off = [0]
    for w in IN_SIZES:
        off.append(off[-1] + w)
    col = lambda i: w_in[:, :, off[i]:off[i + 1]]

    sl_dil = jnp.repeat(_alibi_slopes(N_HEADS_DIL), HEAD_DIM).reshape(N_HEADS_DIL // 2, 1, LANES)
    sl_diff = jnp.repeat(_alibi_slopes(N_HEADS_DIFF), 2 * ATT_TQ).reshape(N_HEADS_DIFF // 2, 1, 4 * ATT_TQ)

    gains = norm_gains.reshape(depth, -1, 1, D_MODEL)
    wg = w_ffn_gate.astype(BF16)
    wu = w_ffn_up.astype(BF16)
    wd = w_ffn_down.astype(BF16)
    wn = jnp.concatenate([col(0), col(1), col(2), col(4), col(7)], axis=2).astype(BF16)
    wt = jnp.concatenate([col(3), col(5), col(6), col(8)], axis=2).transpose(0, 2, 1).astype(BF16)
    wo = w_out.astype(BF16)
    sub_gain = diff_subln_gain.reshape(depth, HEAD_DIM, 1)
    tiles_per_step = PROJ_ROWS // ATT_TKV
    assert PROJ_ROWS % ATT_TKV == 0 and tiles_per_step <= SUBLANES

    for layer in range(depth):
        lam_init = 0.8 - 0.6 * math.exp(-0.3 * layer)
        xs = _ffn(xs, gains, wg, wu, wd, layer, 0)
        dil, kb, kc, qbt, vbt, qct, vct, kn = _inproj(xs, gains, wn, wt, layer)
        knorm = kn[:, :tiles_per_step, :N_HEADS_DIFF // 2].reshape(s // ATT_TKV, N_HEADS_DIFF // 2).T
        oa = _dilated(dil, sl_dil)
        ob = _diff_attention(qbt, kb, vbt, knorm, diff_lambda, sub_gain, sl_diff, layer, lam_init)
        oc = _sb_attention(qct, kc, vct)
        xs = _outproj(xs, gains, oa, ob, oc, wo, layer)
        xs = _ffn(xs, gains, wg, wu, wd, layer, 1)
    return xs.reshape(b, s, d)
```

```python
import functools
import math

import jax
import jax.numpy as jnp
from jax import lax
from jax.experimental import pallas as pl
from jax.experimental.pallas import tpu as pltpu

D_MODEL = 1024
D_FF = 2816
HEAD_DIM = 64
N_HEADS_DIL = 6
N_HEADS_DIFF = 4
N_HEADS_SB = 6
DIFF_QK_DIM = HEAD_DIM // 2
DILATED_PAIRS = ((128, 1), (512, 4), (2048, 16))
N_BACK = 128
NORM_EPS = 1e-6
IN_SIZES = (384, 384, 384, 256, 256, 256, 384, 384, 384)

LANES = 128
SUBLANES = 8
VMEM_LIMIT = 56 * 1024 * 1024

FFN_ROWS = 512
FFN_CHUNKS = 11
PROJ_ROWS = 512
DIL_CHUNK = 2048
DIL_UNROLL = 8
ATT_TQ = 256
ATT_TKV = 256
BF16_ROWS = 16
DIFF_ACC_ROWS = HEAD_DIM + BF16_ROWS
DIFF_GROUP = 4
NEG_BIG = -1e30
LOG2E = 1.4426950408889634
DEAD_EXP2 = -160.0

F32 = jnp.float32
BF16 = jnp.bfloat16


def _rms(x, g):
    return x * lax.rsqrt(jnp.mean(x * x, axis=-1, keepdims=True) + NORM_EPS) * g


def _params(*sem):
    return pltpu.CompilerParams(dimension_semantics=sem, vmem_limit_bytes=VMEM_LIMIT)


def _resident(shape):
    zeros = (0,) * len(shape)
    return pl.BlockSpec(shape, lambda *_: zeros, pipeline_mode=pl.Buffered(1))


def _picked(shape, *lead):
    idx = tuple(lead) + (0,) * len(shape)
    return pl.BlockSpec((None,) * len(lead) + tuple(shape), lambda *_: idx, pipeline_mode=pl.Buffered(1))


def _ffn_body(x_ref, gpre_ref, gpost_ref, wg_ref, wu_ref, wd_ref, o_ref):
    x = x_ref[...]
    hb = _rms(x, gpre_ref[...]).astype(BF16)
    step = D_FF // FFN_CHUNKS
    y = None
    for c in range(FFN_CHUNKS):
        sl = slice(c * step, (c + 1) * step)
        g = jnp.dot(hb, wg_ref[:, sl], preferred_element_type=F32)
        u = jnp.dot(hb, wu_ref[:, sl], preferred_element_type=F32)
        a = (g / (1.0 + jnp.exp(-g)) * u).astype(BF16)
        part = jnp.dot(a, wd_ref[sl, :], preferred_element_type=F32)
        y = part if y is None else y + part
    o_ref[...] = x + 0.5 * _rms(y, gpost_ref[...])


def _ffn(x, gains, wg, wu, wd, layer, half):
    s = x.shape[0]
    row = pl.BlockSpec((FFN_ROWS, D_MODEL), lambda i: (i, 0))
    return pl.pallas_call(
        _ffn_body,
        grid=(s // FFN_ROWS,),
        in_specs=[row, _picked((1, D_MODEL), layer, 4 * half), _picked((1, D_MODEL), layer, 4 * half + 1),
                  _picked((D_MODEL, D_FF), layer, half), _picked((D_MODEL, D_FF), layer, half),
                  _picked((D_FF, D_MODEL), layer, half)],
        out_specs=row,
        out_shape=jax.ShapeDtypeStruct((s, D_MODEL), F32),
        compiler_params=_params("parallel"),
        name="ffn",
    )(x, gains, gains, wg, wu, wd)


N_NAT = 384 * 3 + 256 + 384
N_TR = 256 + 256 + 384 + 384


def _inproj_body(x_ref, g_ref, wn_ref, wt_ref, dil_ref, kb_ref, kc_ref, qbt_ref, vbt_ref, qct_ref, vct_ref,
                 kn_ref, run_ref):
    hb = _rms(x_ref[...], g_ref[...]).astype(BF16)
    pn = jnp.dot(hb, wn_ref[...], preferred_element_type=F32)
    dil_ref[:, 0:384] = pn[:, 0:384] * (HEAD_DIM ** -0.5 * LOG2E)
    dil_ref[:, 384:1152] = pn[:, 384:1152]
    kbf = pn[:, 1152:1408].astype(BF16)
    kb_ref[...] = kbf
    kc_ref[...] = pn[:, 1408:1792].astype(BF16)
    pt = lax.dot_general(wt_ref[...], hb, (((1,), (1,)), ((), ())), preferred_element_type=F32)
    qbt_ref[...] = (pt[0:256] * (DIFF_QK_DIM ** -0.5 * LOG2E)).astype(BF16)
    vbt_ref[...] = pt[256:512].astype(BF16)
    qct_ref[...] = (pt[512:896] * (HEAD_DIM ** -0.5 * LOG2E)).astype(BF16)
    vct_ref[...] = pt[896:1280].astype(BF16)

    @pl.when(pl.program_id(0) == 0)
    def _():
        run_ref[...] = jnp.zeros(run_ref.shape, F32)

    k2 = kbf.astype(F32)
    k2 = k2 * k2
    n2 = [jnp.sum(k2[:, p * LANES:(p + 1) * LANES], axis=1, keepdims=True) for p in range(N_HEADS_DIFF // 2)]
    lane = lax.broadcasted_iota(jnp.int32, (1, LANES), 1)
    rowi = lax.broadcasted_iota(jnp.int32, (SUBLANES, LANES), 0)
    run = run_ref[...]
    out = jnp.zeros((SUBLANES, LANES), F32)
    for r in range(PROJ_ROWS // ATT_TKV):
        tmax = [jnp.max(n[r * ATT_TKV:(r + 1) * ATT_TKV], axis=0, keepdims=True) for n in n2]
        run = jnp.maximum(run, jnp.where(lane == 0, tmax[0], jnp.where(lane == 1, tmax[1], 0.0)))
        out = jnp.where(rowi == r, jnp.sqrt(run), out)
    run_ref[...] = run
    kn_ref[0] = out


def _inproj(x, gains, wn, wt, layer):
    s = x.shape[0]
    t = PROJ_ROWS
    nat = lambda w: pl.BlockSpec((t, w), lambda i: (i, 0))
    tr = lambda w: pl.BlockSpec((w, t), lambda i: (0, i))
    return pl.pallas_call(
        _inproj_body,
        grid=(s // t,),
        in_specs=[nat(D_MODEL), _picked((1, D_MODEL), layer, 2), _picked((D_MODEL, N_NAT), layer),
                  _picked((N_TR, D_MODEL), layer)],
        out_specs=[nat(1152), nat(256), nat(384), tr(256), tr(256), tr(384), tr(384),
                   pl.BlockSpec((1, SUBLANES, LANES), lambda i: (i, 0, 0))],
        out_shape=[jax.ShapeDtypeStruct((s, 1152), F32),
                   jax.ShapeDtypeStruct((s, 256), BF16), jax.ShapeDtypeStruct((s, 384), BF16),
                   jax.ShapeDtypeStruct((256, s), BF16), jax.ShapeDtypeStruct((256, s), BF16),
                   jax.ShapeDtypeStruct((384, s), BF16), jax.ShapeDtypeStruct((384, s), BF16),
                   jax.ShapeDtypeStruct((s // t, SUBLANES, LANES), F32)],
        scratch_shapes=[pltpu.VMEM((1, LANES), F32)],
        compiler_params=_params("arbitrary"),
        name="inproj",
    )(x, gains, wn, wt)


def _dil_body(slope_ref, q_ref, kp_ref, kc_ref, vp_ref, vc_ref, o_ref, kbuf, vbuf, oacc, macc, lacc):
    n = pl.program_id(1)
    kbuf[0:DIL_CHUNK, :] = kp_ref[...]
    kbuf[DIL_CHUNK:, :] = kc_ref[...]
    vbuf[0:DIL_CHUNK, :] = vp_ref[...]
    vbuf[DIL_CHUNK:, :] = vc_ref[...]
    slope = slope_ref[0] * LOG2E
    lane = lax.broadcasted_iota(jnp.int32, (1, LANES), 1)
    head0 = lane < HEAD_DIM
    slope0 = jnp.max(jnp.where(head0, slope, 0.0), axis=1, keepdims=True)
    slope1 = jnp.max(jnp.where(head0, 0.0, slope), axis=1, keepdims=True)
    qi = lax.broadcasted_iota(jnp.int32, (N_BACK, 2 * N_BACK), 0)
    ki = lax.broadcasted_iota(jnp.int32, (N_BACK, 2 * N_BACK), 1)
    dist = N_BACK + qi - ki
    band = (dist >= 0) & (dist <= N_BACK)
    distf = dist.astype(F32)
    nt = (((1,), (1,)), ((), ()))

    for bi, (window, d) in enumerate(DILATED_PAIRS):
        per_chunk = DIL_CHUNK // window
        biases = [jnp.where(band, -(sl * float(d)) * distf, -jnp.inf) for sl in (slope0, slope1)]

        def tile(it, carry, d=d, window=window, per_chunk=per_chunk, bi=bi, biases=biases):
            if d == 1:
                c, r = it, 0
            elif per_chunk == 1:
                c, r = 0, it
            else:
                c, r = it // d, it % d
            q_start = c * window + r
            k_start = DIL_CHUNK + q_start - window
            if d == 1:
                rows_q = pl.ds(q_start, N_BACK)
                rows_k = pl.ds(k_start, 2 * N_BACK)
            else:
                rows_q = pl.ds(q_start, N_BACK, stride=d)
                rows_k = pl.ds(k_start, 2 * N_BACK, stride=d)
            q2 = q_ref[rows_q, :]
            k2 = kbuf[rows_k, :].astype(BF16)
            v2 = vbuf[rows_k, :].astype(BF16)
            first = jnp.logical_and(n == 0, c == 0)
            outs, ms, ls = [], [], []
            for hm, bias in zip((head0, jnp.logical_not(head0)), biases):
                qe = jnp.where(hm, q2, 0.0).astype(BF16)
                sc = lax.dot_general(qe, k2, nt, preferred_element_type=F32) + bias
                sc = jnp.concatenate([jnp.where(first, -jnp.inf, sc[:, :N_BACK]), sc[:, N_BACK:]], axis=1)
                m = jnp.max(sc, axis=1, keepdims=True)
                p = jnp.exp2(sc - m)
                ls.append(jnp.sum(p, axis=1, keepdims=True))
                ms.append(m)
                outs.append(jnp.dot(p.astype(BF16), v2, preferred_element_type=F32))
            o_blk = jnp.where(head0, outs[0], outs[1])
            m_blk = jnp.where(head0, ms[0], ms[1])
            l_blk = jnp.where(head0, ls[0], ls[1])
            if bi == 0:
                oacc[rows_q, :] = o_blk
                macc[rows_q, :] = m_blk
                lacc[rows_q, :] = l_blk
            else:
                m_old = macc[rows_q, :]
                m_new = jnp.maximum(m_old, m_blk)
                a = jnp.exp2(m_old - m_new)
                b = jnp.exp2(m_blk - m_new)
                oacc[rows_q, :] = a * oacc[rows_q, :] + b * o_blk
                lacc[rows_q, :] = a * lacc[rows_q, :] + b * l_blk
                macc[rows_q, :] = m_new
            return carry

        lax.fori_loop(0, DIL_CHUNK // N_BACK, tile, 0, unroll=DIL_UNROLL)

    o_ref[...] = (oacc[...] / lacc[...]).astype(BF16)


def _dilated(dil, slopes):
    s = dil.shape[0]
    npairs = N_HEADS_DIL // 2
    blk = lambda col0, prev: pl.BlockSpec(
        (DIL_CHUNK, LANES),
        (lambda p, n: (jnp.maximum(n - 1, 0), col0 + p)) if prev else (lambda p, n: (n, col0 + p)))
    buf = lambda rows: pltpu.VMEM((rows, LANES), F32)
    return pl.pallas_call(
        _dil_body,
        grid=(npairs, s // DIL_CHUNK),
        in_specs=[pl.BlockSpec((1, 1, LANES), lambda p, n: (p, 0, 0)),
                  blk(0, False), blk(3, True), blk(3, False), blk(6, True), blk(6, False)],
        out_specs=pl.BlockSpec((DIL_CHUNK, LANES), lambda p, n: (n, p)),
        out_shape=jax.ShapeDtypeStruct((s, N_HEADS_DIL * HEAD_DIM), BF16),
        scratch_shapes=[buf(2 * DIL_CHUNK), buf(2 * DIL_CHUNK), buf(DIL_CHUNK), buf(DIL_CHUNK), buf(DIL_CHUNK)],
        compiler_params=_params("parallel", "parallel"),
        name="dilated",
    )(slopes, dil, dil, dil, dil, dil)


def _diff_body(lam_init, kn_ref, slope_ref, lam_ref, gain_ref, qt_ref, k_ref, vt_ref, o_ref,
               m_ref, acc_ref, s0_ref, s1_ref):
    tq, tkv = ATT_TQ, ATT_TKV
    pair = pl.program_id(0)
    i = pl.program_id(1)
    qt = qt_ref[...]
    row = lax.broadcasted_iota(jnp.int32, (LANES, tq), 0)
    zero = jnp.zeros_like(qt)
    top = jnp.concatenate(
        [jnp.where((row >= DIFF_QK_DIM * c) & (row < DIFF_QK_DIM * (c + 1)), qt, zero) for c in range(4)], axis=1)
    sl2 = slope_ref[0] * LOG2E
    sl_hi = sl2.astype(BF16).astype(F32)
    sl_lo = sl2 - sl_hi
    brow = lax.broadcasted_iota(jnp.int32, (LANES, 4 * tq), 0)
    bottom = jnp.where(brow == 0, sl_hi, jnp.where(brow == 1, sl_lo, 0.0)).astype(BF16)
    qa = jnp.concatenate([top, bottom], axis=0)
    prow = lax.broadcasted_iota(jnp.int32, (tkv, LANES), 0)
    pcol = lax.broadcasted_iota(jnp.int32, (tkv, LANES), 1)
    kofs = jnp.where(pcol < 2, prow, 0).astype(F32).astype(BF16)
    topf = top.astype(F32)
    qn = jnp.sqrt(jnp.sum(topf * topf, axis=0, keepdims=True))
    kpos = lax.broadcasted_iota(jnp.int32, (tkv, 4 * tq), 0)
    qpos = lax.broadcasted_iota(jnp.int32, (tkv, 4 * tq), 1) & (tq - 1)
    causal = kpos <= qpos

    m_ref[...] = jnp.full(m_ref.shape, NEG_BIG, F32)
    acc_ref[...] = jnp.zeros(acc_ref.shape, F32)
    ones = jnp.ones((DIFF_ACC_ROWS - HEAD_DIM, tkv), BF16)

    def tile_start(j):
        return pl.multiple_of(jnp.maximum(j, 0) * tkv, tkv)

    def scores(j):
        k = k_ref[pl.ds(tile_start(j), tkv), :]
        return jnp.dot(jnp.concatenate([k, kofs], axis=1), qa, preferred_element_type=F32)

    def softmax_step(s, j, masked):
        if masked:
            s = jnp.where(causal, s, -jnp.inf)
        c = jnp.where(j >= 0, sl2 * (j * tkv - i * tq).astype(F32), NEG_BIG)
        m_old = m_ref[...]
        m_new = jnp.maximum(m_old, jnp.max(s, axis=0, keepdims=True) + c)
        pb = jnp.exp2(s - (m_new - c)).astype(BF16)
        alpha = jnp.exp2(m_old - m_new)
        m_ref[...] = m_new
        vt = vt_ref[:, pl.ds(tile_start(j), tkv)]
        h, r = HEAD_DIM, DIFF_ACC_ROWS
        v0 = jnp.concatenate([vt[0:h], ones], axis=0)
        v1 = jnp.concatenate([vt[h:], ones], axis=0)
        acc_ref[0:r, :] = alpha[:, 0:2 * tq] * acc_ref[0:r, :] + jnp.dot(
            v0, pb[:, 0:2 * tq], preferred_element_type=F32)
        acc_ref[r:, :] = alpha[:, 2 * tq:] * acc_ref[r:, :] + jnp.dot(
            v1, pb[:, 2 * tq:], preferred_element_type=F32)

    def alive(j):
        kn = kn_ref[pair, jnp.maximum(j, 0)]
        reach = qn * kn + sl2 * ((j * tkv - i * tq).astype(F32) + (tkv - 1.0)) - m_ref[...]
        return jnp.max(reach) > DEAD_EXP2

    s_diag = scores(i)
    s0_ref[...] = scores(i - 1)
    softmax_step(s_diag, i, True)
    bufs = (s0_ref, s1_ref)

    def more(state):
        u, live = state
        return jnp.logical_and(DIFF_GROUP * u < i, live)

    def tile_group(state):
        u, _ = state
        ja = i - 1 - DIFF_GROUP * u
        live = alive(ja - DIFF_GROUP)
        for t in range(DIFF_GROUP):
            bufs[(t + 1) % 2][...] = scores(ja - t - 1)
            softmax_step(bufs[t % 2][...], ja - t, False)
        return u + 1, live

    lax.while_loop(more, tile_group, (0, alive(i - 1)))

    lp = lam_ref[...]
    lam = (jnp.exp(jnp.sum(lp[0:1] * lp[1:2], axis=1, keepdims=True))
           - jnp.exp(jnp.sum(lp[2:3] * lp[3:4], axis=1, keepdims=True)) + lam_init)
    acc = acc_ref[...]
    outs = []
    for e in range(2):
        a = acc[e * DIFF_ACC_ROWS:e * DIFF_ACC_ROWS + HEAD_DIM]
        inv = 1.0 / acc[e * DIFF_ACC_ROWS + HEAD_DIM:e * DIFF_ACC_ROWS + HEAD_DIM + 1]
        i1 = inv[:, 0:tq]
        i2 = inv[:, tq:]
        o = a[:, 0:tq] * i1 - lam * (a[:, tq:] * i2)
        o = o * lax.rsqrt(jnp.mean(o * o, axis=0, keepdims=True) + NORM_EPS)
        outs.append(o * gain_ref[...] * (1.0 - lam_init))
    o_ref[...] = jnp.concatenate(outs, axis=0).T.astype(BF16)


def _diff_attention(qbt, kb, vbt, knorm, lam_params, gain_col, slopes, layer, lam_init):
    s = kb.shape[0]
    tq = ATT_TQ
    npairs = N_HEADS_DIFF // 2
    return pl.pallas_call(
        functools.partial(_diff_body, lam_init),
        grid=(npairs, s // tq),
        in_specs=[pl.BlockSpec(memory_space=pltpu.SMEM),
                  pl.BlockSpec((1, 1, 4 * tq), lambda p, i: (p, 0, 0)),
                  pl.BlockSpec((None, 4, DIFF_QK_DIM), lambda p, i: (layer, 0, 0)),
                  pl.BlockSpec((None, HEAD_DIM, 1), lambda p, i: (layer, 0, 0)),
                  pl.BlockSpec((LANES, tq), lambda p, i: (p, i)),
                  pl.BlockSpec((s, LANES), lambda p, i: (0, p)),
                  pl.BlockSpec((LANES, s), lambda p, i: (p, 0))],
        out_specs=pl.BlockSpec((tq, LANES), lambda p, i: (i, p)),
        out_shape=jax.ShapeDtypeStruct((s, N_HEADS_DIFF * HEAD_DIM), BF16),
        scratch_shapes=[pltpu.VMEM((1, 4 * tq), F32), pltpu.VMEM((2 * DIFF_ACC_ROWS, 2 * tq), F32),
                        pltpu.VMEM((ATT_TKV, 4 * tq), F32), pltpu.VMEM((ATT_TKV, 4 * tq), F32)],
        compiler_params=_params("parallel", "arbitrary"),
        name="diff_attn",
    )(knorm, slopes, lam_params, gain_col, qbt, kb, vbt)


def _sb_body(qt_ref, k_ref, vt_ref, o_ref, carry_ref, acc_ref):
    tq, tkv = ATT_TQ, ATT_TKV
    i = pl.program_id(1)
    qt = qt_ref[...]
    row = lax.broadcasted_iota(jnp.int32, (LANES, tq), 0)
    zero = jnp.zeros_like(qt)
    qa = jnp.concatenate([jnp.where(row < HEAD_DIM, qt, zero), jnp.where(row >= HEAD_DIM, qt, zero)], axis=1)
    ur = lax.broadcasted_iota(jnp.int32, (tkv, tkv), 0)
    uc = lax.broadcasted_iota(jnp.int32, (tkv, tkv), 1)
    upper = (uc >= ur).astype(BF16)
    kpos = lax.broadcasted_iota(jnp.int32, (tkv, 2 * tq), 0)
    qpos = lax.broadcasted_iota(jnp.int32, (tkv, 2 * tq), 1) & (tq - 1)
    causal = kpos < qpos

    carry_ref[...] = jnp.zeros(carry_ref.shape, F32)
    acc_ref[...] = jnp.zeros(acc_ref.shape, F32)

    def tile(j, masked, guarded=False):
        ks = pl.multiple_of(jnp.maximum(j, 0) * tkv, tkv)
        k = k_ref[pl.ds(ks, tkv), :]
        z = jnp.dot(k, qa, preferred_element_type=F32)
        if guarded:
            z = jnp.where(j >= 0, z, NEG_BIG)
        sp = jnp.maximum(z, 0.0) + jnp.log(1.0 + jnp.exp2(-jnp.abs(z))) * LOG2E
        if masked:
            sp = jnp.where(causal, sp, 0.0)
        hi = sp.astype(BF16)
        lo = (sp - hi.astype(F32)).astype(BF16)
        cs = (jnp.dot(upper, hi, preferred_element_type=F32)
              + jnp.dot(upper, lo, preferred_element_type=F32))
        w = jnp.exp2((z + carry_ref[...]) - cs)
        if masked:
            w = jnp.where(causal, w, 0.0)
        wb = w.astype(BF16)
        carry_ref[...] = carry_ref[...] - cs[0:1]
        vt = vt_ref[:, pl.ds(ks, tkv)]
        h = HEAD_DIM
        acc_ref[0:h, :] = acc_ref[0:h, :] + jnp.dot(vt[0:h], wb[:, 0:tq], preferred_element_type=F32)
        acc_ref[h:, :] = acc_ref[h:, :] + jnp.dot(vt[h:], wb[:, tq:], preferred_element_type=F32)

    def alive():
        return jnp.max(carry_ref[...]) > DEAD_EXP2

    tile(i, True)
    tile(i - 1, False, guarded=True)

    def more(state):
        it, live = state
        return jnp.logical_and(it < i, live)

    def full_tile(state):
        it, _ = state
        tile(i - 1 - it, False)
        return it + 1, alive()

    lax.while_loop(more, full_tile, (1, alive()))
    o_ref[...] = acc_ref[...].T.astype(BF16)


def _sb_attention(qct, kc, vct):
    s = kc.shape[0]
    tq = ATT_TQ
    npairs = N_HEADS_SB // 2
    return pl.pallas_call(
        _sb_body,
        grid=(npairs, s // tq),
        in_specs=[pl.BlockSpec((LANES, tq), lambda p, i: (p, i)),
                  pl.BlockSpec((s, LANES), lambda p, i: (0, p)),
                  pl.BlockSpec((LANES, s), lambda p, i: (p, 0))],
        out_specs=pl.BlockSpec((tq, LANES), lambda p, i: (i, p)),
        out_shape=jax.ShapeDtypeStruct((s, N_HEADS_SB * HEAD_DIM), BF16),
        scratch_shapes=[pltpu.VMEM((1, 2 * tq), F32), pltpu.VMEM((LANES, tq), F32)],
        compiler_params=_params("parallel", "arbitrary"),
        name="sb_attn",
    )(qct, kc, vct)


def _outproj_body(x_ref, g_ref, oa_ref, ob_ref, oc_ref, w_ref, o_ref):
    a, b = N_HEADS_DIL * HEAD_DIM, (N_HEADS_DIL + N_HEADS_DIFF) * HEAD_DIM
    y = (jnp.dot(oa_ref[...], w_ref[0:a, :], preferred_element_type=F32)
         + jnp.dot(ob_ref[...], w_ref[a:b, :], preferred_element_type=F32)
         + jnp.dot(oc_ref[...], w_ref[b:, :], preferred_element_type=F32))
    o_ref[...] = x_ref[...] + _rms(y, g_ref[...])


def _outproj(x, gains, oa, ob, oc, wo, layer):
    s = x.shape[0]
    t = PROJ_ROWS
    nat = lambda w: pl.BlockSpec((t, w), lambda i: (i, 0))
    return pl.pallas_call(
        _outproj_body,
        grid=(s // t,),
        in_specs=[nat(D_MODEL), _picked((1, D_MODEL), layer, 3), nat(384), nat(256), nat(384),
                  _picked((D_MODEL, D_MODEL), layer)],
        out_specs=nat(D_MODEL),
        out_shape=jax.ShapeDtypeStruct((s, D_MODEL), F32),
        compiler_params=_params("parallel"),
        name="outproj",
    )(x, gains, oa, ob, oc, wo)


def _alibi_slopes(n):
    return 2.0 ** (-8.0 * jnp.arange(1, n + 1, dtype=F32) / n)


def kernel(x, norm_gains, w_ffn_gate, w_ffn_up, w_ffn_down, w_in, w_out, diff_lambda, diff_subln_gain):
    b, s, d = x.shape
    assert b == 1 and d == D_MODEL and s % DIL_CHUNK == 0 and ATT_TQ == ATT_TKV
    depth = norm_gains.shape[0]
    xs = x.reshape(s, d)

    off = [0]
    for w in IN_SIZES:
        off.append(off[-1] + w)
    col = lambda i: w_in[:, :, off[i]:off[i + 1]]

    sl_dil = jnp.repeat(_alibi_slopes(N_HEADS_DIL), HEAD_DIM).reshape(N_HEADS_DIL // 2, 1, LANES)
    sl_diff = jnp.repeat(_alibi_slopes(N_HEADS_DIFF), 2 * ATT_TQ).reshape(N_HEADS_DIFF // 2, 1, 4 * ATT_TQ)

    gains = norm_gains.reshape(depth, -1, 1, D_MODEL)
    wg = w_ffn_gate.astype(BF16)
    wu = w_ffn_up.astype(BF16)
    wd = w_ffn_down.astype(BF16)
    wn = jnp.concatenate([col(0), col(1), col(2), col(4), col(7)], axis=2).astype(BF16)
    wt = jnp.concatenate([col(3), col(5), col(6), col(8)], axis=2).transpose(0, 2, 1).astype(BF16)
    wo = w_out.astype(BF16)
    sub_gain = diff_subln_gain.reshape(depth, HEAD_DIM, 1)
    tiles_per_step = PROJ_ROWS // ATT_TKV
    assert PROJ_ROWS % ATT_TKV == 0 and tiles_per_step <= SUBLANES

    for layer in range(depth):
        lam_init = 0.8 - 0.6 * math.exp(-0.3 * layer)
        xs = _ffn(xs, gains, wg, wu, wd, layer, 0)
        dil, kb, kc, qbt, vbt, qct, vct, kn = _inproj(xs, gains, wn, wt, layer)
        knorm = kn[:, :tiles_per_step, :N_HEADS_DIFF // 2].reshape(s // ATT_TKV, N_HEADS_DIFF // 2).T
        oa = _dilated(dil, sl_dil)
        ob = _diff_attention(qbt, kb, vbt, knorm, diff_lambda, sub_gain, sl_diff, layer, lam_init)
        oc = _sb_attention(qct, kc, vct)
        xs = _outproj(xs, gains, oa, ob, oc, wo, layer)
        xs = _ffn(xs, gains, wg, wu, wd, layer, 1)
    return xs.reshape(b, s, d)
```

```python
import functools
import math

import jax
import jax.numpy as jnp
from jax import lax
from jax.experimental import pallas as pl
from jax.experimental.pallas import tpu as pltpu

D_MODEL = 1024
D_FF = 2816
HEAD_DIM = 64
N_HEADS_DIL = 6
N_HEADS_DIFF = 4
N_HEADS_SB = 6
DIFF_QK_DIM = HEAD_DIM // 2
DILATED_PAIRS = ((128, 1), (512, 4), (2048, 16))
N_BACK = 128
NORM_EPS = 1e-6
IN_SIZES = (384, 384, 384, 256, 256, 256, 384, 384, 384)

LANES = 128
SUBLANES = 8
VMEM_LIMIT = 56 * 1024 * 1024

FFN_ROWS = 512
FFN_CHUNKS = 11
PROJ_ROWS = 512
DIL_CHUNK = 2048
DIL_UNROLL = 8
ATT_TQ = 256
ATT_TKV = 256
BF16_ROWS = 16
DIFF_ACC_ROWS = HEAD_DIM + BF16_ROWS
DIFF_GROUP = 4
SB_SUB = 4
NEG_BIG = -1e30
LOG2E = 1.4426950408889634
DEAD_EXP2 = -160.0

F32 = jnp.float32
BF16 = jnp.bfloat16


def _rms(x, g):
    return x * lax.rsqrt(jnp.mean(x * x, axis=-1, keepdims=True) + NORM_EPS) * g


def _params(*sem):
    return pltpu.CompilerParams(dimension_semantics=sem, vmem_limit_bytes=VMEM_LIMIT)


def _resident(shape):
    zeros = (0,) * len(shape)
    return pl.BlockSpec(shape, lambda *_: zeros, pipeline_mode=pl.Buffered(1))


def _picked(shape, *lead):
    idx = tuple(lead) + (0,) * len(shape)
    return pl.BlockSpec((None,) * len(lead) + tuple(shape), lambda *_: idx, pipeline_mode=pl.Buffered(1))


def _ffn_body(x_ref, gpre_ref, gpost_ref, wg_ref, wu_ref, wd_ref, o_ref):
    x = x_ref[...]
    hb = _rms(x, gpre_ref[...]).astype(BF16)
    step = D_FF // FFN_CHUNKS
    y = None
    for c in range(FFN_CHUNKS):
        sl = slice(c * step, (c + 1) * step)
        g = jnp.dot(hb, wg_ref[:, sl], preferred_element_type=F32)
        u = jnp.dot(hb, wu_ref[:, sl], preferred_element_type=F32)
        a = (g / (1.0 + jnp.exp(-g)) * u).astype(BF16)
        part = jnp.dot(a, wd_ref[sl, :], preferred_element_type=F32)
        y = part if y is None else y + part
    o_ref[...] = x + 0.5 * _rms(y, gpost_ref[...])


def _ffn(x, gains, wg, wu, wd, layer, half):
    s = x.shape[0]
    row = pl.BlockSpec((FFN_ROWS, D_MODEL), lambda i: (i, 0))
    return pl.pallas_call(
        _ffn_body,
        grid=(s // FFN_ROWS,),
        in_specs=[row, _picked((1, D_MODEL), layer, 4 * half), _picked((1, D_MODEL), layer, 4 * half + 1),
                  _picked((D_MODEL, D_FF), layer, half), _picked((D_MODEL, D_FF), layer, half),
                  _picked((D_FF, D_MODEL), layer, half)],
        out_specs=row,
        out_shape=jax.ShapeDtypeStruct((s, D_MODEL), F32),
        compiler_params=_params("parallel"),
        name="ffn",
    )(x, gains, gains, wg, wu, wd)


N_NAT = 384 * 3 + 256 + 384
N_TR = 256 + 256 + 384 + 384


def _inproj_body(x_ref, g_ref, wn_ref, wt_ref, dil_ref, kb_ref, kc_ref, qbt_ref, vbt_ref, qct_ref, vct_ref,
                 kn_ref, run_ref):
    hb = _rms(x_ref[...], g_ref[...]).astype(BF16)
    pn = jnp.dot(hb, wn_ref[...], preferred_element_type=F32)
    dil_ref[:, 0:384] = pn[:, 0:384] * (HEAD_DIM ** -0.5 * LOG2E)
    dil_ref[:, 384:1152] = pn[:, 384:1152]
    kbf = pn[:, 1152:1408].astype(BF16)
    kb_ref[...] = kbf
    kc_ref[...] = pn[:, 1408:1792].astype(BF16)
    pt = lax.dot_general(wt_ref[...], hb, (((1,), (1,)), ((), ())), preferred_element_type=F32)
    qbt_ref[...] = (pt[0:256] * (DIFF_QK_DIM ** -0.5 * LOG2E)).astype(BF16)
    vbt_ref[...] = pt[256:512].astype(BF16)
    qct_ref[...] = (pt[512:896] * (HEAD_DIM ** -0.5 * LOG2E)).astype(BF16)
    vct_ref[...] = pt[896:1280].astype(BF16)

    @pl.when(pl.program_id(0) == 0)
    def _():
        run_ref[...] = jnp.zeros(run_ref.shape, F32)

    k2 = kbf.astype(F32)
    k2 = k2 * k2
    n2 = [jnp.sum(k2[:, p * LANES:(p + 1) * LANES], axis=1, keepdims=True) for p in range(N_HEADS_DIFF // 2)]
    lane = lax.broadcasted_iota(jnp.int32, (1, LANES), 1)
    rowi = lax.broadcasted_iota(jnp.int32, (SUBLANES, LANES), 0)
    run = run_ref[...]
    out = jnp.zeros((SUBLANES, LANES), F32)
    for r in range(PROJ_ROWS // ATT_TKV):
        tmax = [jnp.max(n[r * ATT_TKV:(r + 1) * ATT_TKV], axis=0, keepdims=True) for n in n2]
        run = jnp.maximum(run, jnp.where(lane == 0, tmax[0], jnp.where(lane == 1, tmax[1], 0.0)))
        out = jnp.where(rowi == r, jnp.sqrt(run), out)
    run_ref[...] = run
    kn_ref[0] = out


def _inproj(x, gains, wn, wt, layer):
    s = x.shape[0]
    t = PROJ_ROWS
    nat = lambda w: pl.BlockSpec((t, w), lambda i: (i, 0))
    tr = lambda w: pl.BlockSpec((w, t), lambda i: (0, i))
    return pl.pallas_call(
        _inproj_body,
        grid=(s // t,),
        in_specs=[nat(D_MODEL), _picked((1, D_MODEL), layer, 2), _picked((D_MODEL, N_NAT), layer),
                  _picked((N_TR, D_MODEL), layer)],
        out_specs=[nat(1152), nat(256), nat(384), tr(256), tr(256), tr(384), tr(384),
                   pl.BlockSpec((1, SUBLANES, LANES), lambda i: (i, 0, 0))],
        out_shape=[jax.ShapeDtypeStruct((s, 1152), F32),
                   jax.ShapeDtypeStruct((s, 256), BF16), jax.ShapeDtypeStruct((s, 384), BF16),
                   jax.ShapeDtypeStruct((256, s), BF16), jax.ShapeDtypeStruct((256, s), BF16),
                   jax.ShapeDtypeStruct((384, s), BF16), jax.ShapeDtypeStruct((384, s), BF16),
                   jax.ShapeDtypeStruct((s // t, SUBLANES, LANES), F32)],
        scratch_shapes=[pltpu.VMEM((1, LANES), F32)],
        compiler_params=_params("arbitrary"),
        name="inproj",
    )(x, gains, wn, wt)


def _dil_body(slope_ref, q_ref, kp_ref, kc_ref, vp_ref, vc_ref, o_ref, kbuf, vbuf, oacc, macc, lacc):
    n = pl.program_id(1)
    kbuf[0:DIL_CHUNK, :] = kp_ref[...]
    kbuf[DIL_CHUNK:, :] = kc_ref[...]
    vbuf[0:DIL_CHUNK, :] = vp_ref[...]
    vbuf[DIL_CHUNK:, :] = vc_ref[...]
    slope = slope_ref[0] * LOG2E
    lane = lax.broadcasted_iota(jnp.int32, (1, LANES), 1)
    head0 = lane < HEAD_DIM
    slope0 = jnp.max(jnp.where(head0, slope, 0.0), axis=1, keepdims=True)
    slope1 = jnp.max(jnp.where(head0, 0.0, slope), axis=1, keepdims=True)
    qi = lax.broadcasted_iota(jnp.int32, (N_BACK, 2 * N_BACK), 0)
    ki = lax.broadcasted_iota(jnp.int32, (N_BACK, 2 * N_BACK), 1)
    dist = N_BACK + qi - ki
    band = (dist >= 0) & (dist <= N_BACK)
    distf = dist.astype(F32)
    nt = (((1,), (1,)), ((), ()))

    for bi, (window, d) in enumerate(DILATED_PAIRS):
        per_chunk = DIL_CHUNK // window
        biases = [jnp.where(band, -(sl * float(d)) * distf, -jnp.inf) for sl in (slope0, slope1)]

        def tile(it, carry, d=d, window=window, per_chunk=per_chunk, bi=bi, biases=biases):
            if d == 1:
                c, r = it, 0
            elif per_chunk == 1:
                c, r = 0, it
            else:
                c, r = it // d, it % d
            q_start = c * window + r
            k_start = DIL_CHUNK + q_start - window
            if d == 1:
                rows_q = pl.ds(q_start, N_BACK)
                rows_k = pl.ds(k_start, 2 * N_BACK)
            else:
                rows_q = pl.ds(q_start, N_BACK, stride=d)
                rows_k = pl.ds(k_start, 2 * N_BACK, stride=d)
            q2 = q_ref[rows_q, :]
            k2 = kbuf[rows_k, :].astype(BF16)
            v2 = vbuf[rows_k, :].astype(BF16)
            first = jnp.logical_and(n == 0, c == 0)
            outs, ms, ls = [], [], []
            for hm, bias in zip((head0, jnp.logical_not(head0)), biases):
                qe = jnp.where(hm, q2, 0.0).astype(BF16)
                sc = lax.dot_general(qe, k2, nt, preferred_element_type=F32) + bias
                sc = jnp.concatenate([jnp.where(first, -jnp.inf, sc[:, :N_BACK]), sc[:, N_BACK:]], axis=1)
                m = jnp.max(sc, axis=1, keepdims=True)
                p = jnp.exp2(sc - m)
                ls.append(jnp.sum(p, axis=1, keepdims=True))
                ms.append(m)
                outs.append(jnp.dot(p.astype(BF16), v2, preferred_element_type=F32))
            o_blk = jnp.where(head0, outs[0], outs[1])
            m_blk = jnp.where(head0, ms[0], ms[1])
            l_blk = jnp.where(head0, ls[0], ls[1])
            if bi == 0:
                oacc[rows_q, :] = o_blk
                macc[rows_q, :] = m_blk
                lacc[rows_q, :] = l_blk
            else:
                m_old = macc[rows_q, :]
                m_new = jnp.maximum(m_old, m_blk)
                a = jnp.exp2(m_old - m_new)
                b = jnp.exp2(m_blk - m_new)
                oacc[rows_q, :] = a * oacc[rows_q, :] + b * o_blk
                lacc[rows_q, :] = a * lacc[rows_q, :] + b * l_blk
                macc[rows_q, :] = m_new
            return carry

        lax.fori_loop(0, DIL_CHUNK // N_BACK, tile, 0, unroll=DIL_UNROLL)

    o_ref[...] = (oacc[...] / lacc[...]).astype(BF16)


def _dilated(dil, slopes):
    s = dil.shape[0]
    npairs = N_HEADS_DIL // 2
    blk = lambda col0, prev: pl.BlockSpec(
        (DIL_CHUNK, LANES),
        (lambda p, n: (jnp.maximum(n - 1, 0), col0 + p)) if prev else (lambda p, n: (n, col0 + p)))
    buf = lambda rows: pltpu.VMEM((rows, LANES), F32)
    return pl.pallas_call(
        _dil_body,
        grid=(npairs, s // DIL_CHUNK),
        in_specs=[pl.BlockSpec((1, 1, LANES), lambda p, n: (p, 0, 0)),
                  blk(0, False), blk(3, True), blk(3, False), blk(6, True), blk(6, False)],
        out_specs=pl.BlockSpec((DIL_CHUNK, LANES), lambda p, n: (n, p)),
        out_shape=jax.ShapeDtypeStruct((s, N_HEADS_DIL * HEAD_DIM), BF16),
        scratch_shapes=[buf(2 * DIL_CHUNK), buf(2 * DIL_CHUNK), buf(DIL_CHUNK), buf(DIL_CHUNK), buf(DIL_CHUNK)],
        compiler_params=_params("parallel", "parallel"),
        name="dilated",
    )(slopes, dil, dil, dil, dil, dil)


def _diff_body(lam_init, kn_ref, slope_ref, lam_ref, gain_ref, qt_ref, k_ref, vt_ref, o_ref,
               m_ref, acc_ref, s0_ref, s1_ref):
    tq, tkv = ATT_TQ, ATT_TKV
    pair = pl.program_id(0)
    i = pl.program_id(1)
    qt = qt_ref[...]
    row = lax.broadcasted_iota(jnp.int32, (LANES, tq), 0)
    zero = jnp.zeros_like(qt)
    top = jnp.concatenate(
        [jnp.where((row >= DIFF_QK_DIM * c) & (row < DIFF_QK_DIM * (c + 1)), qt, zero) for c in range(4)], axis=1)
    sl2 = slope_ref[0] * LOG2E
    sl_hi = sl2.astype(BF16).astype(F32)
    sl_lo = sl2 - sl_hi
    brow = lax.broadcasted_iota(jnp.int32, (LANES, 4 * tq), 0)
    bottom = jnp.where(brow == 0, sl_hi, jnp.where(brow == 1, sl_lo, 0.0)).astype(BF16)
    qa = jnp.concatenate([top, bottom], axis=0)
    prow = lax.broadcasted_iota(jnp.int32, (tkv, LANES), 0)
    pcol = lax.broadcasted_iota(jnp.int32, (tkv, LANES), 1)
    kofs = jnp.where(pcol < 2, prow, 0).astype(F32).astype(BF16)
    topf = top.astype(F32)
    qn = jnp.sqrt(jnp.sum(topf * topf, axis=0, keepdims=True))
    kpos = lax.broadcasted_iota(jnp.int32, (tkv, 4 * tq), 0)
    qpos = lax.broadcasted_iota(jnp.int32, (tkv, 4 * tq), 1) & (tq - 1)
    causal = kpos <= qpos

    m_ref[...] = jnp.full(m_ref.shape, NEG_BIG, F32)
    acc_ref[...] = jnp.zeros(acc_ref.shape, F32)
    ones = jnp.ones((DIFF_ACC_ROWS - HEAD_DIM, tkv), BF16)

    def tile_start(j):
        return pl.multiple_of(jnp.maximum(j, 0) * tkv, tkv)

    def scores(j):
        k = k_ref[pl.ds(tile_start(j), tkv), :]
        return jnp.dot(jnp.concatenate([k, kofs], axis=1), qa, preferred_element_type=F32)

    def softmax_step(s, j, masked):
        if masked:
            s = jnp.where(causal, s, -jnp.inf)
        c = jnp.where(j >= 0, sl2 * (j * tkv - i * tq).astype(F32), NEG_BIG)
        m_old = m_ref[...]
        m_new = jnp.maximum(m_old, jnp.max(s, axis=0, keepdims=True) + c)
        pb = jnp.exp2(s - (m_new - c)).astype(BF16)
        alpha = jnp.exp2(m_old - m_new)
        m_ref[...] = m_new
        vt = vt_ref[:, pl.ds(tile_start(j), tkv)]
        h, r = HEAD_DIM, DIFF_ACC_ROWS
        v0 = jnp.concatenate([vt[0:h], ones], axis=0)
        v1 = jnp.concatenate([vt[h:], ones], axis=0)
        acc_ref[0:r, :] = alpha[:, 0:2 * tq] * acc_ref[0:r, :] + jnp.dot(
            v0, pb[:, 0:2 * tq], preferred_element_type=F32)
        acc_ref[r:, :] = alpha[:, 2 * tq:] * acc_ref[r:, :] + jnp.dot(
            v1, pb[:, 2 * tq:], preferred_element_type=F32)

    def alive(j):
        kn = kn_ref[pair, jnp.maximum(j, 0)]
        reach = qn * kn + sl2 * ((j * tkv - i * tq).astype(F32) + (tkv - 1.0)) - m_ref[...]
        return jnp.max(reach) > DEAD_EXP2

    s_diag = scores(i)
    s0_ref[...] = scores(i - 1)
    softmax_step(s_diag, i, True)
    bufs = (s0_ref, s1_ref)

    def more(state):
        u, live = state
        return jnp.logical_and(DIFF_GROUP * u < i, live)

    def tile_group(state):
        u, _ = state
        ja = i - 1 - DIFF_GROUP * u
        live = alive(ja - DIFF_GROUP)
        for t in range(DIFF_GROUP):
            bufs[(t + 1) % 2][...] = scores(ja - t - 1)
            softmax_step(bufs[t % 2][...], ja - t, False)
        return u + 1, live

    lax.while_loop(more, tile_group, (0, alive(i - 1)))

    lp = lam_ref[...]
    lam = (jnp.exp(jnp.sum(lp[0:1] * lp[1:2], axis=1, keepdims=True))
           - jnp.exp(jnp.sum(lp[2:3] * lp[3:4], axis=1, keepdims=True)) + lam_init)
    acc = acc_ref[...]
    outs = []
    for e in range(2):
        a = acc[e * DIFF_ACC_ROWS:e * DIFF_ACC_ROWS + HEAD_DIM]
        inv = 1.0 / acc[e * DIFF_ACC_ROWS + HEAD_DIM:e * DIFF_ACC_ROWS + HEAD_DIM + 1]
        i1 = inv[:, 0:tq]
        i2 = inv[:, tq:]
        o = a[:, 0:tq] * i1 - lam * (a[:, tq:] * i2)
        o = o * lax.rsqrt(jnp.mean(o * o, axis=0, keepdims=True) + NORM_EPS)
        outs.append(o * gain_ref[...] * (1.0 - lam_init))
    o_ref[...] = jnp.concatenate(outs, axis=0).T.astype(BF16)


def _diff_attention(qbt, kb, vbt, knorm, lam_params, gain_col, slopes, layer, lam_init):
    s = kb.shape[0]
    tq = ATT_TQ
    npairs = N_HEADS_DIFF // 2
    return pl.pallas_call(
        functools.partial(_diff_body, lam_init),
        grid=(npairs, s // tq),
        in_specs=[pl.BlockSpec(memory_space=pltpu.SMEM),
                  pl.BlockSpec((1, 1, 4 * tq), lambda p, i: (p, 0, 0)),
                  pl.BlockSpec((None, 4, DIFF_QK_DIM), lambda p, i: (layer, 0, 0)),
                  pl.BlockSpec((None, HEAD_DIM, 1), lambda p, i: (layer, 0, 0)),
                  pl.BlockSpec((LANES, tq), lambda p, i: (p, i)),
                  pl.BlockSpec((s, LANES), lambda p, i: (0, p)),
                  pl.BlockSpec((LANES, s), lambda p, i: (p, 0))],
        out_specs=pl.BlockSpec((tq, LANES), lambda p, i: (i, p)),
        out_shape=jax.ShapeDtypeStruct((s, N_HEADS_DIFF * HEAD_DIM), BF16),
        scratch_shapes=[pltpu.VMEM((1, 4 * tq), F32), pltpu.VMEM((2 * DIFF_ACC_ROWS, 2 * tq), F32),
                        pltpu.VMEM((ATT_TKV, 4 * tq), F32), pltpu.VMEM((ATT_TKV, 4 * tq), F32)],
        compiler_params=_params("parallel", "arbitrary"),
        name="diff_attn",
    )(knorm, slopes, lam_params, gain_col, qbt, kb, vbt)


def _sb_body(qt_ref, k_ref, vt_ref, o_ref, carry_ref, acc_ref):
    tq, tkv = ATT_TQ, ATT_TKV
    step = pl.program_id(1)
    row = lax.broadcasted_iota(jnp.int32, (LANES, tq), 0)
    ur = lax.broadcasted_iota(jnp.int32, (tkv, tkv), 0)
    uc = lax.broadcasted_iota(jnp.int32, (tkv, tkv), 1)
    upper = (uc >= ur).astype(BF16)
    kpos = lax.broadcasted_iota(jnp.int32, (tkv, 2 * tq), 0)
    qpos = lax.broadcasted_iota(jnp.int32, (tkv, 2 * tq), 1) & (tq - 1)
    causal = kpos < qpos

    carry_ref[...] = jnp.zeros(carry_ref.shape, F32)
    acc_ref[...] = jnp.zeros(acc_ref.shape, F32)
    qas = []
    for s in range(SB_SUB):
        qt = qt_ref[:, s * tq:(s + 1) * tq]
        zero = jnp.zeros_like(qt)
        qas.append(jnp.concatenate([jnp.where(row < HEAD_DIM, qt, zero), jnp.where(row >= HEAD_DIM, qt, zero)], axis=1))

    def stages(s, j, masked, guarded):
        v = {}

        def logits():
            v["ks"] = pl.multiple_of(jnp.maximum(j, 0) * tkv, tkv)
            z = jnp.dot(k_ref[pl.ds(v["ks"], tkv), :], qas[s], preferred_element_type=F32)
            v["z"] = jnp.where(j >= 0, z, NEG_BIG) if guarded else z

        def softplus():
            z = v["z"]
            sp = jnp.maximum(z, 0.0) + jnp.log(1.0 + jnp.exp2(-jnp.abs(z))) * LOG2E
            if masked:
                sp = jnp.where(causal, sp, 0.0)
            v["hi"] = sp.astype(BF16)
            v["lo"] = (sp - v["hi"].astype(F32)).astype(BF16)

        def suffix_sum():
            v["cs"] = (jnp.dot(upper, v["hi"], preferred_element_type=F32)
                       + jnp.dot(upper, v["lo"], preferred_element_type=F32))

        def weights():
            w = jnp.exp2((v["z"] + carry_ref[s]) - v["cs"])
            if masked:
                w = jnp.where(causal, w, 0.0)
            v["wb"] = w.astype(BF16)
            carry_ref[s] = carry_ref[s] - v["cs"][0:1]

        def values():
            vt = vt_ref[:, pl.ds(v["ks"], tkv)]
            h, wb = HEAD_DIM, v["wb"]
            acc_ref[s, 0:h, :] = acc_ref[s, 0:h, :] + jnp.dot(vt[0:h], wb[:, 0:tq], preferred_element_type=F32)
            acc_ref[s, h:, :] = acc_ref[s, h:, :] + jnp.dot(vt[h:], wb[:, tq:], preferred_element_type=F32)

        return (logits, softplus, suffix_sum, weights, values)

    chains = ([stages(s, SB_SUB * step + s, True, False) for s in range(SB_SUB)]
              + [stages(s, SB_SUB * step + s - 1, False, True) for s in range(SB_SUB)])
    n_stage = len(chains[0])
    for t in range(len(chains) + n_stage - 1):
        for c, chain in enumerate(chains):
            if 0 <= t - c < n_stage:
                chain[t - c]()

    for s in range(SB_SUB):
        i = SB_SUB * step + s

        def alive(s=s):
            return jnp.max(carry_ref[s]) > DEAD_EXP2

        def more(state, i=i):
            it, live = state
            return jnp.logical_and(it < i, live)

        def full_tile(state, s=s, i=i, alive=alive):
            it, _ = state
            for stage in stages(s, i - 1 - it, False, False):
                stage()
            return it + 1, alive()

        lax.while_loop(more, full_tile, (1, alive()))
        o_ref[s * tq:(s + 1) * tq, :] = acc_ref[s].T.astype(BF16)


def _sb_attention(qct, kc, vct):
    s = kc.shape[0]
    tq = ATT_TQ * SB_SUB
    npairs = N_HEADS_SB // 2
    return pl.pallas_call(
        _sb_body,
        grid=(npairs, s // tq),
        in_specs=[pl.BlockSpec((LANES, tq), lambda p, i: (p, i)),
                  pl.BlockSpec((s, LANES), lambda p, i: (0, p)),
                  pl.BlockSpec((LANES, s), lambda p, i: (p, 0))],
        out_specs=pl.BlockSpec((tq, LANES), lambda p, i: (i, p)),
        out_shape=jax.ShapeDtypeStruct((s, N_HEADS_SB * HEAD_DIM), BF16),
        scratch_shapes=[pltpu.VMEM((SB_SUB, 1, 2 * ATT_TQ), F32), pltpu.VMEM((SB_SUB, LANES, ATT_TQ), F32)],
        compiler_params=_params("parallel", "arbitrary"),
        name="sb_attn",
    )(qct, kc, vct)


def _outproj_body(x_ref, g_ref, oa_ref, ob_ref, oc_ref, w_ref, o_ref):
    a, b = N_HEADS_DIL * HEAD_DIM, (N_HEADS_DIL + N_HEADS_DIFF) * HEAD_DIM
    y = (jnp.dot(oa_ref[...], w_ref[0:a, :], preferred_element_type=F32)
         + jnp.dot(ob_ref[...], w_ref[a:b, :], preferred_element_type=F32)
         + jnp.dot(oc_ref[...], w_ref[b:, :], preferred_element_type=F32))
    o_ref[...] = x_ref[...] + _rms(y, g_ref[...])


def _outproj(x, gains, oa, ob, oc, wo, layer):
    s = x.shape[0]
    t = PROJ_ROWS
    nat = lambda w: pl.BlockSpec((t, w), lambda i: (i, 0))
    return pl.pallas_call(
        _outproj_body,
        grid=(s // t,),
        in_specs=[nat(D_MODEL), _picked((1, D_MODEL), layer, 3), nat(384), nat(256), nat(384),
                  _picked((D_MODEL, D_MODEL), layer)],
        out_specs=nat(D_MODEL),
        out_shape=jax.ShapeDtypeStruct((s, D_MODEL), F32),
        compiler_params=_params("parallel"),
        name="outproj",
    )(x, gains, oa, ob, oc, wo)


def _alibi_slopes(n):
    return 2.0 ** (-8.0 * jnp.arange(1, n + 1, dtype=F32) / n)


def kernel(x, norm_gains, w_ffn_gate, w_ffn_up, w_ffn_down, w_in, w_out, diff_lambda, diff_subln_gain):
    b, s, d = x.shape
    assert b == 1 and d == D_MODEL and s % DIL_CHUNK == 0 and ATT_TQ == ATT_TKV
    depth = norm_gains.shape[0]
    xs = x.reshape(s, d)

    off = [0]
    for w in IN_SIZES:
        off.append(off[-1] + w)
    col = lambda i: w_in[:, :, off[i]:off[i + 1]]

    sl_dil = jnp.repeat(_alibi_slopes(N_HEADS_DIL), HEAD_DIM).reshape(N_HEADS_DIL // 2, 1, LANES)
    sl_diff = jnp.repeat(_alibi_slopes(N_HEADS_DIFF), 2 * ATT_TQ).reshape(N_HEADS_DIFF // 2, 1, 4 * ATT_TQ)

    gains = norm_gains.reshape(depth, -1, 1, D_MODEL)
    wg = w_ffn_gate.astype(BF16)
    wu = w_ffn_up.astype(BF16)
    wd = w_ffn_down.astype(BF16)
    wn = jnp.concatenate([col(0), col(1), col(2), col(4), col(7)], axis=2).astype(BF16)
    wt = jnp.concatenate([col(3), col(5), col(6), col(8)], axis=2).transpose(0, 2, 1).astype(BF16)
    wo = w_out.astype(BF16)
    sub_gain = diff_subln_gain.reshape(depth, HEAD_DIM, 1)
    tiles_per_step = PROJ_ROWS // ATT_TKV
    assert PROJ_ROWS % ATT_TKV == 0 and tiles_per_step <= SUBLANES

    for layer in range(depth):
        lam_init = 0.8 - 0.6 * math.exp(-0.3 * layer)
        xs = _ffn(xs, gains, wg, wu, wd, layer, 0)
        dil, kb, kc, qbt, vbt, qct, vct, kn = _inproj(xs, gains, wn, wt, layer)
        knorm = kn[:, :tiles_per_step, :N_HEADS_DIFF // 2].reshape(s // ATT_TKV, N_HEADS_DIFF // 2).T
        oa = _dilated(dil, sl_dil)
        ob = _diff_attention(qbt, kb, vbt, knorm, diff_lambda, sub_gain, sl_diff, layer, lam_init)
        oc = _sb_attention(qct, kc, vct)
        xs = _outproj(xs, gains, oa, ob, oc, wo, layer)
        xs = _ffn(xs, gains, wg, wu, wd, layer, 1)
    return xs.reshape(b, s, d)
```

```python
import functools
import math

import jax
import jax.numpy as jnp
from jax import lax
from jax.experimental import pallas as pl
from jax.experimental.pallas import tpu as pltpu

D_MODEL = 1024
D_FF = 2816
HEAD_DIM = 64
N_HEADS_DIL = 6
N_HEADS_DIFF = 4
N_HEADS_SB = 6
DIFF_QK_DIM = HEAD_DIM // 2
DILATED_PAIRS = ((128, 1), (512, 4), (2048, 16))
N_BACK = 128
NORM_EPS = 1e-6
IN_SIZES = (384, 384, 384, 256, 256, 256, 384, 384, 384)

LANES = 128
SUBLANES = 8
VMEM_LIMIT = 56 * 1024 * 1024

FFN_ROWS = 512
FFN_CHUNKS = 11
PROJ_ROWS = 512
DIL_CHUNK = 2048
DIL_UNROLL = 8
ATT_TQ = 256
ATT_TKV = 256
BF16_ROWS = 16
DIFF_ACC_ROWS = HEAD_DIM + BF16_ROWS
DIFF_GROUP = 4
SB_SUB = 4
NEG_BIG = -1e30
LOG2E = 1.4426950408889634
DEAD_EXP2 = -160.0

F32 = jnp.float32
BF16 = jnp.bfloat16


def _rms(x, g):
    return x * lax.rsqrt(jnp.mean(x * x, axis=-1, keepdims=True) + NORM_EPS) * g


def _params(*sem):
    return pltpu.CompilerParams(dimension_semantics=sem, vmem_limit_bytes=VMEM_LIMIT)


def _resident(shape):
    zeros = (0,) * len(shape)
    return pl.BlockSpec(shape, lambda *_: zeros, pipeline_mode=pl.Buffered(1))


def _picked(shape, *lead):
    idx = tuple(lead) + (0,) * len(shape)
    return pl.BlockSpec((None,) * len(lead) + tuple(shape), lambda *_: idx, pipeline_mode=pl.Buffered(1))


def _ffn_body(x_ref, gpre_ref, gpost_ref, wg_ref, wu_ref, wd_ref, o_ref):
    x = x_ref[...]
    hb = _rms(x, gpre_ref[...]).astype(BF16)
    step = D_FF // FFN_CHUNKS
    y = None
    for c in range(FFN_CHUNKS):
        sl = slice(c * step, (c + 1) * step)
        g = jnp.dot(hb, wg_ref[:, sl], preferred_element_type=F32)
        u = jnp.dot(hb, wu_ref[:, sl], preferred_element_type=F32)
        a = (g / (1.0 + jnp.exp(-g)) * u).astype(BF16)
        part = jnp.dot(a, wd_ref[sl, :], preferred_element_type=F32)
        y = part if y is None else y + part
    o_ref[...] = x + 0.5 * _rms(y, gpost_ref[...])


def _ffn(x, gains, wg, wu, wd, layer, half):
    s = x.shape[0]
    row = pl.BlockSpec((FFN_ROWS, D_MODEL), lambda i: (i, 0))
    return pl.pallas_call(
        _ffn_body,
        grid=(s // FFN_ROWS,),
        in_specs=[row, _picked((1, D_MODEL), layer, 4 * half), _picked((1, D_MODEL), layer, 4 * half + 1),
                  _picked((D_MODEL, D_FF), layer, half), _picked((D_MODEL, D_FF), layer, half),
                  _picked((D_FF, D_MODEL), layer, half)],
        out_specs=row,
        out_shape=jax.ShapeDtypeStruct((s, D_MODEL), F32),
        compiler_params=_params("parallel"),
        name="ffn",
    )(x, gains, gains, wg, wu, wd)


N_NAT = 384 * 3 + 256 + 384
N_TR = 256 + 256 + 384 + 384


def _inproj_body(x_ref, g_ref, wn_ref, wt_ref, dil_ref, kb_ref, kc_ref, qbt_ref, vbt_ref, qct_ref, vct_ref,
                 kn_ref, run_ref):
    hb = _rms(x_ref[...], g_ref[...]).astype(BF16)
    pn = jnp.dot(hb, wn_ref[...], preferred_element_type=F32)
    dil_ref[:, 0:384] = pn[:, 0:384] * (HEAD_DIM ** -0.5 * LOG2E)
    dil_ref[:, 384:1152] = pn[:, 384:1152]
    kbf = pn[:, 1152:1408].astype(BF16)
    kb_ref[...] = kbf
    kc_ref[...] = pn[:, 1408:1792].astype(BF16)
    pt = lax.dot_general(wt_ref[...], hb, (((1,), (1,)), ((), ())), preferred_element_type=F32)
    qbt_ref[...] = (pt[0:256] * (DIFF_QK_DIM ** -0.5 * LOG2E)).astype(BF16)
    vbt_ref[...] = pt[256:512].astype(BF16)
    qct_ref[...] = (pt[512:896] * (HEAD_DIM ** -0.5 * LOG2E)).astype(BF16)
    vct_ref[...] = pt[896:1280].astype(BF16)

    @pl.when(pl.program_id(0) == 0)
    def _():
        run_ref[...] = jnp.zeros(run_ref.shape, F32)

    k2 = kbf.astype(F32)
    k2 = k2 * k2
    n2 = [jnp.sum(k2[:, p * LANES:(p + 1) * LANES], axis=1, keepdims=True) for p in range(N_HEADS_DIFF // 2)]
    lane = lax.broadcasted_iota(jnp.int32, (1, LANES), 1)
    rowi = lax.broadcasted_iota(jnp.int32, (SUBLANES, LANES), 0)
    run = run_ref[...]
    out = jnp.zeros((SUBLANES, LANES), F32)
    for r in range(PROJ_ROWS // ATT_TKV):
        tmax = [jnp.max(n[r * ATT_TKV:(r + 1) * ATT_TKV], axis=0, keepdims=True) for n in n2]
        run = jnp.maximum(run, jnp.where(lane == 0, tmax[0], jnp.where(lane == 1, tmax[1], 0.0)))
        out = jnp.where(rowi == r, jnp.sqrt(run), out)
    run_ref[...] = run
    kn_ref[0] = out


def _inproj(x, gains, wn, wt, layer):
    s = x.shape[0]
    t = PROJ_ROWS
    nat = lambda w: pl.BlockSpec((t, w), lambda i: (i, 0))
    tr = lambda w: pl.BlockSpec((w, t), lambda i: (0, i))
    return pl.pallas_call(
        _inproj_body,
        grid=(s // t,),
        in_specs=[nat(D_MODEL), _picked((1, D_MODEL), layer, 2), _picked((D_MODEL, N_NAT), layer),
                  _picked((N_TR, D_MODEL), layer)],
        out_specs=[nat(1152), nat(256), nat(384), tr(256), tr(256), tr(384), tr(384),
                   pl.BlockSpec((1, SUBLANES, LANES), lambda i: (i, 0, 0))],
        out_shape=[jax.ShapeDtypeStruct((s, 1152), F32),
                   jax.ShapeDtypeStruct((s, 256), BF16), jax.ShapeDtypeStruct((s, 384), BF16),
                   jax.ShapeDtypeStruct((256, s), BF16), jax.ShapeDtypeStruct((256, s), BF16),
                   jax.ShapeDtypeStruct((384, s), BF16), jax.ShapeDtypeStruct((384, s), BF16),
                   jax.ShapeDtypeStruct((s // t, SUBLANES, LANES), F32)],
        scratch_shapes=[pltpu.VMEM((1, LANES), F32)],
        compiler_params=_params("arbitrary"),
        name="inproj",
    )(x, gains, wn, wt)


def _dil_body(slope_ref, q_ref, kp_ref, kc_ref, vp_ref, vc_ref, o_ref, kbuf, vbuf, oacc, macc, lacc):
    n = pl.program_id(1)
    kbuf[0:DIL_CHUNK, :] = kp_ref[...]
    kbuf[DIL_CHUNK:, :] = kc_ref[...]
    vbuf[0:DIL_CHUNK, :] = vp_ref[...]
    vbuf[DIL_CHUNK:, :] = vc_ref[...]
    slope = slope_ref[0] * LOG2E
    lane = lax.broadcasted_iota(jnp.int32, (1, LANES), 1)
    head0 = lane < HEAD_DIM
    slope0 = jnp.max(jnp.where(head0, slope, 0.0), axis=1, keepdims=True)
    slope1 = jnp.max(jnp.where(head0, 0.0, slope), axis=1, keepdims=True)
    qi = lax.broadcasted_iota(jnp.int32, (N_BACK, 2 * N_BACK), 0)
    ki = lax.broadcasted_iota(jnp.int32, (N_BACK, 2 * N_BACK), 1)
    dist = N_BACK + qi - ki
    band = (dist >= 0) & (dist <= N_BACK)
    distf = dist.astype(F32)
    nt = (((1,), (1,)), ((), ()))

    for bi, (window, d) in enumerate(DILATED_PAIRS):
        per_chunk = DIL_CHUNK // window
        biases = [jnp.where(band, -(sl * float(d)) * distf, -jnp.inf) for sl in (slope0, slope1)]

        def tile(it, carry, d=d, window=window, per_chunk=per_chunk, bi=bi, biases=biases):
            if d == 1:
                c, r = it, 0
            elif per_chunk == 1:
                c, r = 0, it
            else:
                c, r = it // d, it % d
            q_start = c * window + r
            k_start = DIL_CHUNK + q_start - window
            if d == 1:
                rows_q = pl.ds(q_start, N_BACK)
                rows_k = pl.ds(k_start, 2 * N_BACK)
            else:
                rows_q = pl.ds(q_start, N_BACK, stride=d)
                rows_k = pl.ds(k_start, 2 * N_BACK, stride=d)
            q2 = q_ref[rows_q, :]
            k2 = kbuf[rows_k, :].astype(BF16)
            v2 = vbuf[rows_k, :].astype(BF16)
            first = jnp.logical_and(n == 0, c == 0)
            outs, ms, ls = [], [], []
            for hm, bias in zip((head0, jnp.logical_not(head0)), biases):
                qe = jnp.where(hm, q2, 0.0).astype(BF16)
                sc = lax.dot_general(qe, k2, nt, preferred_element_type=F32) + bias
                sc = jnp.concatenate([jnp.where(first, -jnp.inf, sc[:, :N_BACK]), sc[:, N_BACK:]], axis=1)
                m = jnp.max(sc, axis=1, keepdims=True)
                p = jnp.exp2(sc - m)
                ls.append(jnp.sum(p, axis=1, keepdims=True))
                ms.append(m)
                outs.append(jnp.dot(p.astype(BF16), v2, preferred_element_type=F32))
            o_blk = jnp.where(head0, outs[0], outs[1])
            m_blk = jnp.where(head0, ms[0], ms[1])
            l_blk = jnp.where(head0, ls[0], ls[1])
            if bi == 0:
                oacc[rows_q, :] = o_blk
                macc[rows_q, :] = m_blk
                lacc[rows_q, :] = l_blk
            else:
                m_old = macc[rows_q, :]
                m_new = jnp.maximum(m_old, m_blk)
                a = jnp.exp2(m_old - m_new)
                b = jnp.exp2(m_blk - m_new)
                oacc[rows_q, :] = a * oacc[rows_q, :] + b * o_blk
                lacc[rows_q, :] = a * lacc[rows_q, :] + b * l_blk
                macc[rows_q, :] = m_new
            return carry

        lax.fori_loop(0, DIL_CHUNK // N_BACK, tile, 0, unroll=DIL_UNROLL)

    o_ref[...] = (oacc[...] / lacc[...]).astype(BF16)


def _dilated(dil, slopes):
    s = dil.shape[0]
    npairs = N_HEADS_DIL // 2
    blk = lambda col0, prev: pl.BlockSpec(
        (DIL_CHUNK, LANES),
        (lambda p, n: (jnp.maximum(n - 1, 0), col0 + p)) if prev else (lambda p, n: (n, col0 + p)))
    buf = lambda rows: pltpu.VMEM((rows, LANES), F32)
    return pl.pallas_call(
        _dil_body,
        grid=(npairs, s // DIL_CHUNK),
        in_specs=[pl.BlockSpec((1, 1, LANES), lambda p, n: (p, 0, 0)),
                  blk(0, False), blk(3, True), blk(3, False), blk(6, True), blk(6, False)],
        out_specs=pl.BlockSpec((DIL_CHUNK, LANES), lambda p, n: (n, p)),
        out_shape=jax.ShapeDtypeStruct((s, N_HEADS_DIL * HEAD_DIM), BF16),
        scratch_shapes=[buf(2 * DIL_CHUNK), buf(2 * DIL_CHUNK), buf(DIL_CHUNK), buf(DIL_CHUNK), buf(DIL_CHUNK)],
        compiler_params=_params("parallel", "parallel"),
        name="dilated",
    )(slopes, dil, dil, dil, dil, dil)


def _diff_body(lam_init, kn_ref, slope_ref, lam_ref, gain_ref, qt_ref, k_ref, vt_ref, o_ref,
               m_ref, acc_ref, s0_ref, s1_ref):
    tq, tkv = ATT_TQ, ATT_TKV
    pair = pl.program_id(0)
    i = pl.program_id(1)
    qt = qt_ref[...]
    row = lax.broadcasted_iota(jnp.int32, (LANES, tq), 0)
    zero = jnp.zeros_like(qt)
    top = jnp.concatenate(
        [jnp.where((row >= DIFF_QK_DIM * c) & (row < DIFF_QK_DIM * (c + 1)), qt, zero) for c in range(4)], axis=1)
    sl2 = slope_ref[0] * LOG2E
    sl_hi = sl2.astype(BF16).astype(F32)
    sl_lo = sl2 - sl_hi
    brow = lax.broadcasted_iota(jnp.int32, (LANES, 4 * tq), 0)
    bottom = jnp.where(brow == 0, sl_hi, jnp.where(brow == 1, sl_lo, 0.0)).astype(BF16)
    qa = jnp.concatenate([top, bottom], axis=0)
    prow = lax.broadcasted_iota(jnp.int32, (tkv, LANES), 0)
    pcol = lax.broadcasted_iota(jnp.int32, (tkv, LANES), 1)
    kofs = jnp.where(pcol < 2, prow, 0).astype(F32).astype(BF16)
    topf = top.astype(F32)
    qn = jnp.sqrt(jnp.sum(topf * topf, axis=0, keepdims=True))
    kpos = lax.broadcasted_iota(jnp.int32, (tkv, 4 * tq), 0)
    qpos = lax.broadcasted_iota(jnp.int32, (tkv, 4 * tq), 1) & (tq - 1)
    causal = kpos <= qpos

    m_ref[...] = jnp.full(m_ref.shape, NEG_BIG, F32)
    acc_ref[...] = jnp.zeros(acc_ref.shape, F32)
    ones = jnp.ones((DIFF_ACC_ROWS - HEAD_DIM, tkv), BF16)

    def tile_start(j):
        return pl.multiple_of(jnp.maximum(j, 0) * tkv, tkv)

    def lanes(heads):
        return slice(2 * tq * heads[0], 2 * tq * (heads[-1] + 1))

    def scores(j, heads):
        k = k_ref[pl.ds(tile_start(j), tkv), :]
        return jnp.dot(jnp.concatenate([k, kofs], axis=1), qa[:, lanes(heads)], preferred_element_type=F32)

    def softmax_step(s, j, masked, heads):
        ln = lanes(heads)
        if masked:
            s = jnp.where(causal[:, ln], s, -jnp.inf)
        c = jnp.where(j >= 0, sl2[:, ln] * (j * tkv - i * tq).astype(F32), NEG_BIG)
        m_old = m_ref[:, ln]
        m_new = jnp.maximum(m_old, jnp.max(s, axis=0, keepdims=True) + c)
        pb = jnp.exp2(s - (m_new - c)).astype(BF16)
        alpha = jnp.exp2(m_old - m_new)
        m_ref[:, ln] = m_new
        vt = vt_ref[:, pl.ds(tile_start(j), tkv)]
        h, r = HEAD_DIM, DIFF_ACC_ROWS
        for n, e in enumerate(heads):
            ve = jnp.concatenate([vt[e * h:(e + 1) * h], ones], axis=0)
            cols = slice(2 * tq * n, 2 * tq * (n + 1))
            acc_ref[e * r:(e + 1) * r, :] = alpha[:, cols] * acc_ref[e * r:(e + 1) * r, :] + jnp.dot(
                ve, pb[:, cols], preferred_element_type=F32)

    def alive(j, heads):
        ln = lanes(heads)
        kn = kn_ref[pair, jnp.maximum(j, 0)]
        reach = qn[:, ln] * kn + sl2[:, ln] * ((j * tkv - i * tq).astype(F32) + (tkv - 1.0)) - m_ref[:, ln]
        return jnp.max(reach) > DEAD_EXP2

    def sweep(j_first, heads, watch, prefetched):
        width = 2 * tq * len(heads)
        bufs = (s0_ref, s1_ref)
        if not prefetched:
            s0_ref[:, 0:width] = scores(j_first, heads)

        def more(state):
            u, live = state
            return jnp.logical_and(DIFF_GROUP * u <= j_first, live)

        def tile_group(state):
            u, _ = state
            ja = j_first - DIFF_GROUP * u
            live = alive(ja - DIFF_GROUP, watch)
            for t in range(DIFF_GROUP):
                bufs[(t + 1) % 2][:, 0:width] = scores(ja - t - 1, heads)
                softmax_step(bufs[t % 2][:, 0:width], ja - t, False, heads)
            return u + 1, live

        groups, _ = lax.while_loop(more, tile_group, (0, alive(j_first, watch)))
        return j_first - DIFF_GROUP * groups

    both = (0, 1)
    s_diag = scores(i, both)
    s0_ref[...] = scores(i - 1, both)
    softmax_step(s_diag, i, True, both)
    j_rest = sweep(i - 1, both, (0,), True)
    sweep(j_rest, (1,), (1,), False)

    lp = lam_ref[...]
    lam = (jnp.exp(jnp.sum(lp[0:1] * lp[1:2], axis=1, keepdims=True))
           - jnp.exp(jnp.sum(lp[2:3] * lp[3:4], axis=1, keepdims=True)) + lam_init)
    acc = acc_ref[...]
    outs = []
    for e in range(2):
        a = acc[e * DIFF_ACC_ROWS:e * DIFF_ACC_ROWS + HEAD_DIM]
        inv = 1.0 / acc[e * DIFF_ACC_ROWS + HEAD_DIM:e * DIFF_ACC_ROWS + HEAD_DIM + 1]
        i1 = inv[:, 0:tq]
        i2 = inv[:, tq:]
        o = a[:, 0:tq] * i1 - lam * (a[:, tq:] * i2)
        o = o * lax.rsqrt(jnp.mean(o * o, axis=0, keepdims=True) + NORM_EPS)
        outs.append(o * gain_ref[...] * (1.0 - lam_init))
    o_ref[...] = jnp.concatenate(outs, axis=0).T.astype(BF16)


def _diff_attention(qbt, kb, vbt, knorm, lam_params, gain_col, slopes, layer, lam_init):
    s = kb.shape[0]
    tq = ATT_TQ
    npairs = N_HEADS_DIFF // 2
    return pl.pallas_call(
        functools.partial(_diff_body, lam_init),
        grid=(npairs, s // tq),
        in_specs=[pl.BlockSpec(memory_space=pltpu.SMEM),
                  pl.BlockSpec((1, 1, 4 * tq), lambda p, i: (p, 0, 0)),
                  pl.BlockSpec((None, 4, DIFF_QK_DIM), lambda p, i: (layer, 0, 0)),
                  pl.BlockSpec((None, HEAD_DIM, 1), lambda p, i: (layer, 0, 0)),
                  pl.BlockSpec((LANES, tq), lambda p, i: (p, i)),
                  pl.BlockSpec((s, LANES), lambda p, i: (0, p)),
                  pl.BlockSpec((LANES, s), lambda p, i: (p, 0))],
        out_specs=pl.BlockSpec((tq, LANES), lambda p, i: (i, p)),
        out_shape=jax.ShapeDtypeStruct((s, N_HEADS_DIFF * HEAD_DIM), BF16),
        scratch_shapes=[pltpu.VMEM((1, 4 * tq), F32), pltpu.VMEM((2 * DIFF_ACC_ROWS, 2 * tq), F32),
                        pltpu.VMEM((ATT_TKV, 4 * tq), F32), pltpu.VMEM((ATT_TKV, 4 * tq), F32)],
        compiler_params=_params("parallel", "arbitrary"),
        name="diff_attn",
    )(knorm, slopes, lam_params, gain_col, qbt, kb, vbt)


def _sb_body(qt_ref, k_ref, vt_ref, o_ref, carry_ref, acc_ref):
    tq, tkv = ATT_TQ, ATT_TKV
    step = pl.program_id(1)
    row = lax.broadcasted_iota(jnp.int32, (LANES, tq), 0)
    ur = lax.broadcasted_iota(jnp.int32, (tkv, tkv), 0)
    uc = lax.broadcasted_iota(jnp.int32, (tkv, tkv), 1)
    upper = (uc >= ur).astype(BF16)
    kpos = lax.broadcasted_iota(jnp.int32, (tkv, 2 * tq), 0)
    qpos = lax.broadcasted_iota(jnp.int32, (tkv, 2 * tq), 1) & (tq - 1)
    causal = kpos < qpos

    carry_ref[...] = jnp.zeros(carry_ref.shape, F32)
    acc_ref[...] = jnp.zeros(acc_ref.shape, F32)
    qas = []
    for s in range(SB_SUB):
        qt = qt_ref[:, s * tq:(s + 1) * tq]
        zero = jnp.zeros_like(qt)
        qas.append(jnp.concatenate([jnp.where(row < HEAD_DIM, qt, zero), jnp.where(row >= HEAD_DIM, qt, zero)], axis=1))

    def stages(s, j, masked, guarded):
        v = {}

        def logits():
            v["ks"] = pl.multiple_of(jnp.maximum(j, 0) * tkv, tkv)
            z = jnp.dot(k_ref[pl.ds(v["ks"], tkv), :], qas[s], preferred_element_type=F32)
            v["z"] = jnp.where(j >= 0, z, NEG_BIG) if guarded else z

        def softplus():
            z = v["z"]
            sp = jnp.maximum(z, 0.0) + jnp.log(1.0 + jnp.exp2(-jnp.abs(z))) * LOG2E
            if masked:
                sp = jnp.where(causal, sp, 0.0)
            v["hi"] = sp.astype(BF16)
            v["lo"] = (sp - v["hi"].astype(F32)).astype(BF16)

        def suffix_sum():
            v["cs"] = (jnp.dot(upper, v["hi"], preferred_element_type=F32)
                       + jnp.dot(upper, v["lo"], preferred_element_type=F32))

        def weights():
            w = jnp.exp2((v["z"] + carry_ref[s]) - v["cs"])
            if masked:
                w = jnp.where(causal, w, 0.0)
            v["wb"] = w.astype(BF16)
            carry_ref[s] = carry_ref[s] - v["cs"][0:1]

        def values():
            vt = vt_ref[:, pl.ds(v["ks"], tkv)]
            h, wb = HEAD_DIM, v["wb"]
            acc_ref[s, 0:h, :] = acc_ref[s, 0:h, :] + jnp.dot(vt[0:h], wb[:, 0:tq], preferred_element_type=F32)
            acc_ref[s, h:, :] = acc_ref[s, h:, :] + jnp.dot(vt[h:], wb[:, tq:], preferred_element_type=F32)

        return (logits, softplus, suffix_sum, weights, values)

    chains = ([stages(s, SB_SUB * step + s, True, False) for s in range(SB_SUB)]
              + [stages(s, SB_SUB * step + s - 1, False, True) for s in range(SB_SUB)])
    n_stage = len(chains[0])
    for t in range(len(chains) + n_stage - 1):
        for c, chain in enumerate(chains):
            if 0 <= t - c < n_stage:
                chain[t - c]()

    for s in range(SB_SUB):
        i = SB_SUB * step + s

        def alive(s=s):
            return jnp.max(carry_ref[s]) > DEAD_EXP2

        def more(state, i=i):
            it, live = state
            return jnp.logical_and(it < i, live)

        def full_tile(state, s=s, i=i, alive=alive):
            it, _ = state
            for stage in stages(s, i - 1 - it, False, False):
                stage()
            return it + 1, alive()

        lax.while_loop(more, full_tile, (1, alive()))
        o_ref[s * tq:(s + 1) * tq, :] = acc_ref[s].T.astype(BF16)


def _sb_attention(qct, kc, vct):
    s = kc.shape[0]
    tq = ATT_TQ * SB_SUB
    npairs = N_HEADS_SB // 2
    return pl.pallas_call(
        _sb_body,
        grid=(npairs, s // tq),
        in_specs=[pl.BlockSpec((LANES, tq), lambda p, i: (p, i)),
                  pl.BlockSpec((s, LANES), lambda p, i: (0, p)),
                  pl.BlockSpec((LANES, s), lambda p, i: (p, 0))],
        out_specs=pl.BlockSpec((tq, LANES), lambda p, i: (i, p)),
        out_shape=jax.ShapeDtypeStruct((s, N_HEADS_SB * HEAD_DIM), BF16),
        scratch_shapes=[pltpu.VMEM((SB_SUB, 1, 2 * ATT_TQ), F32), pltpu.VMEM((SB_SUB, LANES, ATT_TQ), F32)],
        compiler_params=_params("parallel", "arbitrary"),
        name="sb_attn",
    )(qct, kc, vct)


def _outproj_body(x_ref, g_ref, oa_ref, ob_ref, oc_ref, w_ref, o_ref):
    a, b = N_HEADS_DIL * HEAD_DIM, (N_HEADS_DIL + N_HEADS_DIFF) * HEAD_DIM
    y = (jnp.dot(oa_ref[...], w_ref[0:a, :], preferred_element_type=F32)
         + jnp.dot(ob_ref[...], w_ref[a:b, :], preferred_element_type=F32)
         + jnp.dot(oc_ref[...], w_ref[b:, :], preferred_element_type=F32))
    o_ref[...] = x_ref[...] + _rms(y, g_ref[...])


def _outproj(x, gains, oa, ob, oc, wo, layer):
    s = x.shape[0]
    t = PROJ_ROWS
    nat = lambda w: pl.BlockSpec((t, w), lambda i: (i, 0))
    return pl.pallas_call(
        _outproj_body,
        grid=(s // t,),
        in_specs=[nat(D_MODEL), _picked((1, D_MODEL), layer, 3), nat(384), nat(256), nat(384),
                  _picked((D_MODEL, D_MODEL), layer)],
        out_specs=nat(D_MODEL),
        out_shape=jax.ShapeDtypeStruct((s, D_MODEL), F32),
        compiler_params=_params("parallel"),
        name="outproj",
    )(x, gains, oa, ob, oc, wo)


def _alibi_slopes(n):
    return 2.0 ** (-8.0 * jnp.arange(1, n + 1, dtype=F32) / n)


def kernel(x, norm_gains, w_ffn_gate, w_ffn_up, w_ffn_down, w_in, w_out, diff_lambda, diff_subln_gain):
    b, s, d = x.shape
    assert b == 1 and d == D_MODEL and s % DIL_CHUNK == 0 and ATT_TQ == ATT_TKV
    depth = norm_gains.shape[0]
    xs = x.reshape(s, d)

    off = [0]
    for w in IN_SIZES:
        off.append(off[-1] + w)
    col = lambda i: w_in[:, :, off[i]:off[i + 1]]

    sl_dil = jnp.repeat(_alibi_slopes(N_HEADS_DIL), HEAD_DIM).reshape(N_HEADS_DIL // 2, 1, LANES)
    sl_diff = jnp.repeat(_alibi_slopes(N_HEADS_DIFF), 2 * ATT_TQ).reshape(N_HEADS_DIFF // 2, 1, 4 * ATT_TQ)

    gains = norm_gains.reshape(depth, -1, 1, D_MODEL)
    wg = w_ffn_gate.astype(BF16)
    wu = w_ffn_up.astype(BF16)
    wd = w_ffn_down.astype(BF16)
    wn = jnp.concatenate([col(0), col(1), col(2), col(4), col(7)], axis=2).astype(BF16)
    wt = jnp.concatenate([col(3), col(5), col(6), col(8)], axis=2).transpose(0, 2, 1).astype(BF16)
    wo = w_out.astype(BF16)
    sub_gain = diff_subln_gain.reshape(depth, HEAD_DIM, 1)
    tiles_per_step = PROJ_ROWS // ATT_TKV
    assert PROJ_ROWS % ATT_TKV == 0 and tiles_per_step <= SUBLANES

    for layer in range(depth):
        lam_init = 0.8 - 0.6 * math.exp(-0.3 * layer)
        xs = _ffn(xs, gains, wg, wu, wd, layer, 0)
        dil, kb, kc, qbt, vbt, qct, vct, kn = _inproj(xs, gains, wn, wt, layer)
        knorm = kn[:, :tiles_per_step, :N_HEADS_DIFF // 2].reshape(s // ATT_TKV, N_HEADS_DIFF // 2).T
        oa = _dilated(dil, sl_dil)
        ob = _diff_attention(qbt, kb, vbt, knorm, diff_lambda, sub_gain, sl_diff, layer, lam_init)
        oc = _sb_attention(qct, kc, vct)
        xs = _outproj(xs, gains, oa, ob, oc, wo, layer)
        xs = _ffn(xs, gains, wg, wu, wd, layer, 1)
    return xs.reshape(b, s, d)
```

```python
import functools
import math

import jax
import jax.numpy as jnp
from jax import lax
from jax.experimental import pallas as pl
from jax.experimental.pallas import tpu as pltpu

D_MODEL = 1024
D_FF = 2816
HEAD_DIM = 64
N_HEADS_DIL = 6
N_HEADS_DIFF = 4
N_HEADS_SB = 6
DIFF_QK_DIM = HEAD_DIM // 2
DILATED_PAIRS = ((128, 1), (512, 4), (2048, 16))
N_BACK = 128
NORM_EPS = 1e-6
IN_SIZES = (384, 384, 384, 256, 256, 256, 384, 384, 384)

LANES = 128
SUBLANES = 8
VMEM_LIMIT = 56 * 1024 * 1024

FFN_ROWS = 512
FFN_CHUNKS = 11
PROJ_ROWS = 512
DIL_CHUNK = 2048
DIL_UNROLL = 16
ATT_TQ = 256
ATT_TKV = 256
BF16_ROWS = 16
DIFF_ACC_ROWS = HEAD_DIM + BF16_ROWS
DIFF_GROUP = 4
SB_SUB = 4
NEG_BIG = -1e30
LOG2E = 1.4426950408889634
DEAD_EXP2 = -160.0

F32 = jnp.float32
BF16 = jnp.bfloat16


def _rms(x, g):
    return x * lax.rsqrt(jnp.mean(x * x, axis=-1, keepdims=True) + NORM_EPS) * g


def _params(*sem):
    return pltpu.CompilerParams(dimension_semantics=sem, vmem_limit_bytes=VMEM_LIMIT)


def _resident(shape):
    zeros = (0,) * len(shape)
    return pl.BlockSpec(shape, lambda *_: zeros, pipeline_mode=pl.Buffered(1))


def _picked(shape, *lead):
    idx = tuple(lead) + (0,) * len(shape)
    return pl.BlockSpec((None,) * len(lead) + tuple(shape), lambda *_: idx, pipeline_mode=pl.Buffered(1))


def _ffn_body(x_ref, gpre_ref, gpost_ref, wg_ref, wu_ref, wd_ref, o_ref):
    x = x_ref[...]
    hb = _rms(x, gpre_ref[...]).astype(BF16)
    step = D_FF // FFN_CHUNKS
    y = None
    for c in range(FFN_CHUNKS):
        sl = slice(c * step, (c + 1) * step)
        g = jnp.dot(hb, wg_ref[:, sl], preferred_element_type=F32)
        u = jnp.dot(hb, wu_ref[:, sl], preferred_element_type=F32)
        a = (g / (1.0 + jnp.exp(-g)) * u).astype(BF16)
        part = jnp.dot(a, wd_ref[sl, :], preferred_element_type=F32)
        y = part if y is None else y + part
    o_ref[...] = x + 0.5 * _rms(y, gpost_ref[...])


def _ffn(x, gains, wg, wu, wd, layer, half):
    s = x.shape[0]
    row = pl.BlockSpec((FFN_ROWS, D_MODEL), lambda i: (i, 0))
    return pl.pallas_call(
        _ffn_body,
        grid=(s // FFN_ROWS,),
        in_specs=[row, _picked((1, D_MODEL), layer, 4 * half), _picked((1, D_MODEL), layer, 4 * half + 1),
                  _picked((D_MODEL, D_FF), layer, half), _picked((D_MODEL, D_FF), layer, half),
                  _picked((D_FF, D_MODEL), layer, half)],
        out_specs=row,
        out_shape=jax.ShapeDtypeStruct((s, D_MODEL), F32),
        compiler_params=_params("parallel"),
        name="ffn",
    )(x, gains, gains, wg, wu, wd)


N_NAT = 384 * 3 + 256 + 384
N_TR = 256 + 256 + 384 + 384


def _inproj_body(x_ref, g_ref, wn_ref, wt_ref, dil_ref, kb_ref, kc_ref, qbt_ref, vbt_ref, qct_ref, vct_ref,
                 kn_ref, run_ref):
    hb = _rms(x_ref[...], g_ref[...]).astype(BF16)
    pn = jnp.dot(hb, wn_ref[...], preferred_element_type=F32)
    dil_ref[:, 0:384] = pn[:, 0:384] * (HEAD_DIM ** -0.5 * LOG2E)
    dil_ref[:, 384:1152] = pn[:, 384:1152]
    kbf = pn[:, 1152:1408].astype(BF16)
    kb_ref[...] = kbf
    kc_ref[...] = pn[:, 1408:1792].astype(BF16)
    pt = lax.dot_general(wt_ref[...], hb, (((1,), (1,)), ((), ())), preferred_element_type=F32)
    qbt_ref[...] = (pt[0:256] * (DIFF_QK_DIM ** -0.5 * LOG2E)).astype(BF16)
    vbt_ref[...] = pt[256:512].astype(BF16)
    qct_ref[...] = (pt[512:896] * (HEAD_DIM ** -0.5 * LOG2E)).astype(BF16)
    vct_ref[...] = pt[896:1280].astype(BF16)

    @pl.when(pl.program_id(0) == 0)
    def _():
        run_ref[...] = jnp.zeros(run_ref.shape, F32)

    k2 = kbf.astype(F32)
    k2 = k2 * k2
    n2 = [jnp.sum(k2[:, p * LANES:(p + 1) * LANES], axis=1, keepdims=True) for p in range(N_HEADS_DIFF // 2)]
    lane = lax.broadcasted_iota(jnp.int32, (1, LANES), 1)
    rowi = lax.broadcasted_iota(jnp.int32, (SUBLANES, LANES), 0)
    run = run_ref[...]
    out = jnp.zeros((SUBLANES, LANES), F32)
    for r in range(PROJ_ROWS // ATT_TKV):
        tmax = [jnp.max(n[r * ATT_TKV:(r + 1) * ATT_TKV], axis=0, keepdims=True) for n in n2]
        run = jnp.maximum(run, jnp.where(lane == 0, tmax[0], jnp.where(lane == 1, tmax[1], 0.0)))
        out = jnp.where(rowi == r, jnp.sqrt(run), out)
    run_ref[...] = run
    kn_ref[0] = out


def _inproj(x, gains, wn, wt, layer):
    s = x.shape[0]
    t = PROJ_ROWS
    nat = lambda w: pl.BlockSpec((t, w), lambda i: (i, 0))
    tr = lambda w: pl.BlockSpec((w, t), lambda i: (0, i))
    return pl.pallas_call(
        _inproj_body,
        grid=(s // t,),
        in_specs=[nat(D_MODEL), _picked((1, D_MODEL), layer, 2), _picked((D_MODEL, N_NAT), layer),
                  _picked((N_TR, D_MODEL), layer)],
        out_specs=[nat(1152), nat(256), nat(384), tr(256), tr(256), tr(384), tr(384),
                   pl.BlockSpec((1, SUBLANES, LANES), lambda i: (i, 0, 0))],
        out_shape=[jax.ShapeDtypeStruct((s, 1152), F32),
                   jax.ShapeDtypeStruct((s, 256), BF16), jax.ShapeDtypeStruct((s, 384), BF16),
                   jax.ShapeDtypeStruct((256, s), BF16), jax.ShapeDtypeStruct((256, s), BF16),
                   jax.ShapeDtypeStruct((384, s), BF16), jax.ShapeDtypeStruct((384, s), BF16),
                   jax.ShapeDtypeStruct((s // t, SUBLANES, LANES), F32)],
        scratch_shapes=[pltpu.VMEM((1, LANES), F32)],
        compiler_params=_params("arbitrary"),
        name="inproj",
    )(x, gains, wn, wt)


def _dil_body(slope_ref, q_ref, kp_ref, kc_ref, vp_ref, vc_ref, o_ref, kbuf, vbuf, oacc, macc, lacc):
    n = pl.program_id(1)
    kbuf[0:DIL_CHUNK, :] = kp_ref[...]
    kbuf[DIL_CHUNK:, :] = kc_ref[...]
    vbuf[0:DIL_CHUNK, :] = vp_ref[...]
    vbuf[DIL_CHUNK:, :] = vc_ref[...]
    slope = slope_ref[0] * LOG2E
    lane = lax.broadcasted_iota(jnp.int32, (1, LANES), 1)
    head0 = lane < HEAD_DIM
    slope0 = jnp.max(jnp.where(head0, slope, 0.0), axis=1, keepdims=True)
    slope1 = jnp.max(jnp.where(head0, 0.0, slope), axis=1, keepdims=True)
    qi = lax.broadcasted_iota(jnp.int32, (N_BACK, 2 * N_BACK), 0)
    ki = lax.broadcasted_iota(jnp.int32, (N_BACK, 2 * N_BACK), 1)
    dist = N_BACK + qi - ki
    band = (dist >= 0) & (dist <= N_BACK)
    distf = dist.astype(F32)
    nt = (((1,), (1,)), ((), ()))

    for bi, (window, d) in enumerate(DILATED_PAIRS):
        per_chunk = DIL_CHUNK // window
        biases = [jnp.where(band, -(sl * float(d)) * distf, -jnp.inf) for sl in (slope0, slope1)]

        def tile(it, carry, d=d, window=window, per_chunk=per_chunk, bi=bi, biases=biases):
            if d == 1:
                c, r = it, 0
            elif per_chunk == 1:
                c, r = 0, it
            else:
                c, r = it // d, it % d
            q_start = c * window + r
            k_start = DIL_CHUNK + q_start - window
            if d == 1:
                rows_q = pl.ds(q_start, N_BACK)
                rows_k = pl.ds(k_start, 2 * N_BACK)
            else:
                rows_q = pl.ds(q_start, N_BACK, stride=d)
                rows_k = pl.ds(k_start, 2 * N_BACK, stride=d)
            q2 = q_ref[rows_q, :]
            k2 = kbuf[rows_k, :].astype(BF16)
            v2 = vbuf[rows_k, :].astype(BF16)
            first = jnp.logical_and(n == 0, c == 0)
            outs, ms, ls = [], [], []
            for hm, bias in zip((head0, jnp.logical_not(head0)), biases):
                qe = jnp.where(hm, q2, 0.0).astype(BF16)
                sc = lax.dot_general(qe, k2, nt, preferred_element_type=F32) + bias
                sc = jnp.concatenate([jnp.where(first, -jnp.inf, sc[:, :N_BACK]), sc[:, N_BACK:]], axis=1)
                m = jnp.max(sc, axis=1, keepdims=True)
                p = jnp.exp2(sc - m)
                ls.append(jnp.sum(p, axis=1, keepdims=True))
                ms.append(m)
                outs.append(jnp.dot(p.astype(BF16), v2, preferred_element_type=F32))
            o_blk = jnp.where(head0, outs[0], outs[1])
            m_blk = jnp.where(head0, ms[0], ms[1])
            l_blk = jnp.where(head0, ls[0], ls[1])
            if bi == 0:
                oacc[rows_q, :] = o_blk
                macc[rows_q, :] = m_blk
                lacc[rows_q, :] = l_blk
            else:
                m_old = macc[rows_q, :]
                m_new = jnp.maximum(m_old, m_blk)
                a = jnp.exp2(m_old - m_new)
                b = jnp.exp2(m_blk - m_new)
                oacc[rows_q, :] = a * oacc[rows_q, :] + b * o_blk
                lacc[rows_q, :] = a * lacc[rows_q, :] + b * l_blk
                macc[rows_q, :] = m_new
            return carry

        lax.fori_loop(0, DIL_CHUNK // N_BACK, tile, 0, unroll=DIL_UNROLL)

    o_ref[...] = (oacc[...] / lacc[...]).astype(BF16)


def _dilated(dil, slopes):
    s = dil.shape[0]
    npairs = N_HEADS_DIL // 2
    blk = lambda col0, prev: pl.BlockSpec(
        (DIL_CHUNK, LANES),
        (lambda p, n: (jnp.maximum(n - 1, 0), col0 + p)) if prev else (lambda p, n: (n, col0 + p)))
    buf = lambda rows: pltpu.VMEM((rows, LANES), F32)
    return pl.pallas_call(
        _dil_body,
        grid=(npairs, s // DIL_CHUNK),
        in_specs=[pl.BlockSpec((1, 1, LANES), lambda p, n: (p, 0, 0)),
                  blk(0, False), blk(3, True), blk(3, False), blk(6, True), blk(6, False)],
        out_specs=pl.BlockSpec((DIL_CHUNK, LANES), lambda p, n: (n, p)),
        out_shape=jax.ShapeDtypeStruct((s, N_HEADS_DIL * HEAD_DIM), BF16),
        scratch_shapes=[buf(2 * DIL_CHUNK), buf(2 * DIL_CHUNK), buf(DIL_CHUNK), buf(DIL_CHUNK), buf(DIL_CHUNK)],
        compiler_params=_params("parallel", "parallel"),
        name="dilated",
    )(slopes, dil, dil, dil, dil, dil)


def _diff_body(lam_init, kn_ref, slope_ref, lam_ref, gain_ref, qt_ref, k_ref, vt_ref, o_ref,
               m_ref, acc_ref, s0_ref, s1_ref):
    tq, tkv = ATT_TQ, ATT_TKV
    pair = pl.program_id(0)
    i = pl.program_id(1)
    qt = qt_ref[...]
    row = lax.broadcasted_iota(jnp.int32, (LANES, tq), 0)
    zero = jnp.zeros_like(qt)
    top = jnp.concatenate(
        [jnp.where((row >= DIFF_QK_DIM * c) & (row < DIFF_QK_DIM * (c + 1)), qt, zero) for c in range(4)], axis=1)
    sl2 = slope_ref[0] * LOG2E
    sl_hi = sl2.astype(BF16).astype(F32)
    sl_lo = sl2 - sl_hi
    brow = lax.broadcasted_iota(jnp.int32, (LANES, 4 * tq), 0)
    bottom = jnp.where(brow == 0, sl_hi, jnp.where(brow == 1, sl_lo, 0.0)).astype(BF16)
    qa = jnp.concatenate([top, bottom], axis=0)
    prow = lax.broadcasted_iota(jnp.int32, (tkv, LANES), 0)
    pcol = lax.broadcasted_iota(jnp.int32, (tkv, LANES), 1)
    kofs = jnp.where(pcol < 2, prow, 0).astype(F32).astype(BF16)
    topf = top.astype(F32)
    qn = jnp.sqrt(jnp.sum(topf * topf, axis=0, keepdims=True))
    kpos = lax.broadcasted_iota(jnp.int32, (tkv, 4 * tq), 0)
    qpos = lax.broadcasted_iota(jnp.int32, (tkv, 4 * tq), 1) & (tq - 1)
    causal = kpos <= qpos

    m_ref[...] = jnp.full(m_ref.shape, NEG_BIG, F32)
    acc_ref[...] = jnp.zeros(acc_ref.shape, F32)
    ones = jnp.ones((DIFF_ACC_ROWS - HEAD_DIM, tkv), BF16)

    def tile_start(j):
        return pl.multiple_of(jnp.maximum(j, 0) * tkv, tkv)

    def lanes(heads):
        return slice(2 * tq * heads[0], 2 * tq * (heads[-1] + 1))

    def scores(j, heads):
        k = k_ref[pl.ds(tile_start(j), tkv), :]
        return jnp.dot(jnp.concatenate([k, kofs], axis=1), qa[:, lanes(heads)], preferred_element_type=F32)

    def softmax_step(s, j, masked, heads):
        ln = lanes(heads)
        if masked:
            s = jnp.where(causal[:, ln], s, -jnp.inf)
        c = jnp.where(j >= 0, sl2[:, ln] * (j * tkv - i * tq).astype(F32), NEG_BIG)
        m_old = m_ref[:, ln]
        m_new = jnp.maximum(m_old, jnp.max(s, axis=0, keepdims=True) + c)
        pb = jnp.exp2(s - (m_new - c)).astype(BF16)
        alpha = jnp.exp2(m_old - m_new)
        m_ref[:, ln] = m_new
        vt = vt_ref[:, pl.ds(tile_start(j), tkv)]
        h, r = HEAD_DIM, DIFF_ACC_ROWS
        for n, e in enumerate(heads):
            ve = jnp.concatenate([vt[e * h:(e + 1) * h], ones], axis=0)
            cols = slice(2 * tq * n, 2 * tq * (n + 1))
            acc_ref[e * r:(e + 1) * r, :] = alpha[:, cols] * acc_ref[e * r:(e + 1) * r, :] + jnp.dot(
                ve, pb[:, cols], preferred_element_type=F32)

    def alive(j, heads):
        ln = lanes(heads)
        kn = kn_ref[pair, jnp.maximum(j, 0)]
        reach = qn[:, ln] * kn + sl2[:, ln] * ((j * tkv - i * tq).astype(F32) + (tkv - 1.0)) - m_ref[:, ln]
        return jnp.max(reach) > DEAD_EXP2

    def sweep(j_first, heads, watch):
        ln = lanes(heads)
        bufs = (s0_ref, s1_ref)

        def more(state):
            u, live = state
            return jnp.logical_and(DIFF_GROUP * u <= j_first, live)

        def tile_group(state):
            u, _ = state
            ja = j_first - DIFF_GROUP * u
            live = alive(ja - DIFF_GROUP, watch)
            for t in range(DIFF_GROUP):
                bufs[(t + 1) % 2][:, ln] = scores(ja - t - 1, heads)
                softmax_step(bufs[t % 2][:, ln], ja - t, False, heads)
            return u + 1, live

        assert DIFF_GROUP % 2 == 0
        groups, _ = lax.while_loop(more, tile_group, (0, alive(j_first, watch)))
        return j_first - DIFF_GROUP * groups

    both = (0, 1)
    s_diag = scores(i, both)
    s0_ref[...] = scores(i - 1, both)
    softmax_step(s_diag, i, True, both)
    j_rest = sweep(i - 1, both, (0,))
    sweep(j_rest, (1,), (1,))

    lp = lam_ref[...]
    lam = (jnp.exp(jnp.sum(lp[0:1] * lp[1:2], axis=1, keepdims=True))
           - jnp.exp(jnp.sum(lp[2:3] * lp[3:4], axis=1, keepdims=True)) + lam_init)
    acc = acc_ref[...]
    outs = []
    for e in range(2):
        a = acc[e * DIFF_ACC_ROWS:e * DIFF_ACC_ROWS + HEAD_DIM]
        inv = 1.0 / acc[e * DIFF_ACC_ROWS + HEAD_DIM:e * DIFF_ACC_ROWS + HEAD_DIM + 1]
        i1 = inv[:, 0:tq]
        i2 = inv[:, tq:]
        o = a[:, 0:tq] * i1 - lam * (a[:, tq:] * i2)
        o = o * lax.rsqrt(jnp.mean(o * o, axis=0, keepdims=True) + NORM_EPS)
        outs.append(o * gain_ref[...] * (1.0 - lam_init))
    o_ref[...] = jnp.concatenate(outs, axis=0).T.astype(BF16)


def _diff_attention(qbt, kb, vbt, knorm, lam_params, gain_col, slopes, layer, lam_init):
    s = kb.shape[0]
    tq = ATT_TQ
    npairs = N_HEADS_DIFF // 2
    return pl.pallas_call(
        functools.partial(_diff_body, lam_init),
        grid=(npairs, s // tq),
        in_specs=[pl.BlockSpec(memory_space=pltpu.SMEM),
                  pl.BlockSpec((1, 1, 4 * tq), lambda p, i: (p, 0, 0)),
                  pl.BlockSpec((None, 4, DIFF_QK_DIM), lambda p, i: (layer, 0, 0)),
                  pl.BlockSpec((None, HEAD_DIM, 1), lambda p, i: (layer, 0, 0)),
                  pl.BlockSpec((LANES, tq), lambda p, i: (p, i)),
                  pl.BlockSpec((s, LANES), lambda p, i: (0, p)),
                  pl.BlockSpec((LANES, s), lambda p, i: (p, 0))],
        out_specs=pl.BlockSpec((tq, LANES), lambda p, i: (i, p)),
        out_shape=jax.ShapeDtypeStruct((s, N_HEADS_DIFF * HEAD_DIM), BF16),
        scratch_shapes=[pltpu.VMEM((1, 4 * tq), F32), pltpu.VMEM((2 * DIFF_ACC_ROWS, 2 * tq), F32),
                        pltpu.VMEM((ATT_TKV, 4 * tq), F32), pltpu.VMEM((ATT_TKV, 4 * tq), F32)],
        compiler_params=_params("parallel", "arbitrary"),
        name="diff_attn",
    )(knorm, slopes, lam_params, gain_col, qbt, kb, vbt)


def _sb_body(qt_ref, k_ref, vt_ref, o_ref, carry_ref, acc_ref):
    tq, tkv = ATT_TQ, ATT_TKV
    step = pl.program_id(1)
    row = lax.broadcasted_iota(jnp.int32, (LANES, tq), 0)
    ur = lax.broadcasted_iota(jnp.int32, (tkv, tkv), 0)
    uc = lax.broadcasted_iota(jnp.int32, (tkv, tkv), 1)
    upper = (uc >= ur).astype(BF16)
    kpos = lax.broadcasted_iota(jnp.int32, (tkv, 2 * tq), 0)
    qpos = lax.broadcasted_iota(jnp.int32, (tkv, 2 * tq), 1) & (tq - 1)
    causal = kpos < qpos

    carry_ref[...] = jnp.zeros(carry_ref.shape, F32)
    acc_ref[...] = jnp.zeros(acc_ref.shape, F32)
    qas = []
    for s in range(SB_SUB):
        qt = qt_ref[:, s * tq:(s + 1) * tq]
        zero = jnp.zeros_like(qt)
        qas.append(jnp.concatenate([jnp.where(row < HEAD_DIM, qt, zero), jnp.where(row >= HEAD_DIM, qt, zero)], axis=1))

    def stages(s, j, masked, guarded):
        v = {}

        def logits():
            v["ks"] = pl.multiple_of(jnp.maximum(j, 0) * tkv, tkv)
            z = jnp.dot(k_ref[pl.ds(v["ks"], tkv), :], qas[s], preferred_element_type=F32)
            v["z"] = jnp.where(j >= 0, z, NEG_BIG) if guarded else z

        def softplus():
            z = v["z"]
            sp = jnp.maximum(z, 0.0) + jnp.log(1.0 + jnp.exp2(-jnp.abs(z))) * LOG2E
            if masked:
                sp = jnp.where(causal, sp, 0.0)
            v["hi"] = sp.astype(BF16)
            v["lo"] = (sp - v["hi"].astype(F32)).astype(BF16)

        def suffix_sum():
            v["cs"] = (jnp.dot(upper, v["hi"], preferred_element_type=F32)
                       + jnp.dot(upper, v["lo"], preferred_element_type=F32))

        def weights():
            w = jnp.exp2((v["z"] + carry_ref[s]) - v["cs"])
            if masked:
                w = jnp.where(causal, w, 0.0)
            v["wb"] = w.astype(BF16)
            carry_ref[s] = carry_ref[s] - v["cs"][0:1]

        def values():
            vt = vt_ref[:, pl.ds(v["ks"], tkv)]
            h, wb = HEAD_DIM, v["wb"]
            acc_ref[s, 0:h, :] = acc_ref[s, 0:h, :] + jnp.dot(vt[0:h], wb[:, 0:tq], preferred_element_type=F32)
            acc_ref[s, h:, :] = acc_ref[s, h:, :] + jnp.dot(vt[h:], wb[:, tq:], preferred_element_type=F32)

        return (logits, softplus, suffix_sum, weights, values)

    chains = ([stages(s, SB_SUB * step + s, True, False) for s in range(SB_SUB)]
              + [stages(s, SB_SUB * step + s - 1, False, True) for s in range(SB_SUB)])
    n_stage = len(chains[0])
    for t in range(len(chains) + n_stage - 1):
        for c, chain in enumerate(chains):
            if 0 <= t - c < n_stage:
                chain[t - c]()

    for s in range(SB_SUB):
        i = SB_SUB * step + s

        def alive(s=s):
            return jnp.max(carry_ref[s]) > DEAD_EXP2

        def more(state, i=i):
            it, live = state
            return jnp.logical_and(it < i, live)

        def full_tile(state, s=s, i=i, alive=alive):
            it, _ = state
            for stage in stages(s, i - 1 - it, False, False):
                stage()
            return it + 1, alive()

        lax.while_loop(more, full_tile, (1, alive()))
        o_ref[s * tq:(s + 1) * tq, :] = acc_ref[s].T.astype(BF16)


def _sb_attention(qct, kc, vct):
    s = kc.shape[0]
    tq = ATT_TQ * SB_SUB
    npairs = N_HEADS_SB // 2
    return pl.pallas_call(
        _sb_body,
        grid=(npairs, s // tq),
        in_specs=[pl.BlockSpec((LANES, tq), lambda p, i: (p, i)),
                  pl.BlockSpec((s, LANES), lambda p, i: (0, p)),
                  pl.BlockSpec((LANES, s), lambda p, i: (p, 0))],
        out_specs=pl.BlockSpec((tq, LANES), lambda p, i: (i, p)),
        out_shape=jax.ShapeDtypeStruct((s, N_HEADS_SB * HEAD_DIM), BF16),
        scratch_shapes=[pltpu.VMEM((SB_SUB, 1, 2 * ATT_TQ), F32), pltpu.VMEM((SB_SUB, LANES, ATT_TQ), F32)],
        compiler_params=_params("parallel", "arbitrary"),
        name="sb_attn",
    )(qct, kc, vct)


def _outproj_body(x_ref, g_ref, oa_ref, ob_ref, oc_ref, w_ref, o_ref):
    a, b = N_HEADS_DIL * HEAD_DIM, (N_HEADS_DIL + N_HEADS_DIFF) * HEAD_DIM
    y = (jnp.dot(oa_ref[...], w_ref[0:a, :], preferred_element_type=F32)
         + jnp.dot(ob_ref[...], w_ref[a:b, :], preferred_element_type=F32)
         + jnp.dot(oc_ref[...], w_ref[b:, :], preferred_element_type=F32))
    o_ref[...] = x_ref[...] + _rms(y, g_ref[...])


def _outproj(x, gains, oa, ob, oc, wo, layer):
    s = x.shape[0]
    t = PROJ_ROWS
    nat = lambda w: pl.BlockSpec((t, w), lambda i: (i, 0))
    return pl.pallas_call(
        _outproj_body,
        grid=(s // t,),
        in_specs=[nat(D_MODEL), _picked((1, D_MODEL), layer, 3), nat(384), nat(256), nat(384),
                  _picked((D_MODEL, D_MODEL), layer)],
        out_specs=nat(D_MODEL),
        out_shape=jax.ShapeDtypeStruct((s, D_MODEL), F32),
        compiler_params=_params("parallel"),
        name="outproj",
    )(x, gains, oa, ob, oc, wo)


def _alibi_slopes(n):
    return 2.0 ** (-8.0 * jnp.arange(1, n + 1, dtype=F32) / n)


def kernel(x, norm_gains, w_ffn_gate, w_ffn_up, w_ffn_down, w_in, w_out, diff_lambda, diff_subln_gain):
    b, s, d = x.shape
    assert b == 1 and d == D_MODEL and s % DIL_CHUNK == 0 and ATT_TQ == ATT_TKV
    depth = norm_gains.shape[0]
    xs = x.reshape(s, d)

    off = [0]
    for w in IN_SIZES:
        off.append(off[-1] + w)
    col = lambda i: w_in[:, :, off[i]:off[i + 1]]

    sl_dil = jnp.repeat(_alibi_slopes(N_HEADS_DIL), HEAD_DIM).reshape(N_HEADS_DIL // 2, 1, LANES)
    sl_diff = jnp.repeat(_alibi_slopes(N_HEADS_DIFF), 2 * ATT_TQ).reshape(N_HEADS_DIFF // 2, 1, 4 * ATT_TQ)

    gains = norm_gains.reshape(depth, -1, 1, D_MODEL)
    wg = w_ffn_gate.astype(BF16)
    wu = w_ffn_up.astype(BF16)
    wd = w_ffn_down.astype(BF16)
    wn = jnp.concatenate([col(0), col(1), col(2), col(4), col(7)], axis=2).astype(BF16)
    wt = jnp.concatenate([col(3), col(5), col(6), col(8)], axis=2).transpose(0, 2, 1).astype(BF16)
    wo = w_out.astype(BF16)
    sub_gain = diff_subln_gain.reshape(depth, HEAD_DIM, 1)
    tiles_per_step = PROJ_ROWS // ATT_TKV
    assert PROJ_ROWS % ATT_TKV == 0 and tiles_per_step <= SUBLANES

    for layer in range(depth):
        lam_init = 0.8 - 0.6 * math.exp(-0.3 * layer)
        xs = _ffn(xs, gains, wg, wu, wd, layer, 0)
        dil, kb, kc, qbt, vbt, qct, vct, kn = _inproj(xs, gains, wn, wt, layer)
        knorm = kn[:, :tiles_per_step, :N_HEADS_DIFF // 2].reshape(s // ATT_TKV, N_HEADS_DIFF // 2).T
        oa = _dilated(dil, sl_dil)
        ob = _diff_attention(qbt, kb, vbt, knorm, diff_lambda, sub_gain, sl_diff, layer, lam_init)
        oc = _sb_attention(qct, kc, vct)
        xs = _outproj(xs, gains, oa, ob, oc, wo, layer)
        xs = _ffn(xs, gains, wg, wu, wd, layer, 1)
    return xs.reshape(b, s, d)
```

```python
import functools
import math

import jax
import jax.numpy as jnp
from jax import lax
from jax.experimental import pallas as pl
from jax.experimental.pallas import tpu as pltpu

D_MODEL = 1024
D_FF = 2816
HEAD_DIM = 64
N_HEADS_DIL = 6
N_HEADS_DIFF = 4
N_HEADS_SB = 6
DIFF_QK_DIM = HEAD_DIM // 2
DILATED_PAIRS = ((128, 1), (512, 4), (2048, 16))
N_BACK = 128
NORM_EPS = 1e-6
IN_SIZES = (384, 384, 384, 256, 256, 256, 384, 384, 384)

LANES = 128
SUBLANES = 8
VMEM_LIMIT = 56 * 1024 * 1024

FFN_ROWS = 512
FFN_CHUNKS = 11
PROJ_ROWS = 512
DIL_CHUNK = 2048
DIL_UNROLL = 16
ATT_TQ = 256
ATT_TKV = 256
BF16_ROWS = 16
DIFF_ACC_ROWS = HEAD_DIM + BF16_ROWS
DIFF_GROUP = 4
SB_SUB = 8
NEG_BIG = -1e30
LOG2E = 1.4426950408889634
DEAD_EXP2 = -160.0

F32 = jnp.float32
BF16 = jnp.bfloat16


def _rms(x, g):
    return x * lax.rsqrt(jnp.mean(x * x, axis=-1, keepdims=True) + NORM_EPS) * g


def _params(*sem):
    return pltpu.CompilerParams(dimension_semantics=sem, vmem_limit_bytes=VMEM_LIMIT)


def _resident(shape):
    zeros = (0,) * len(shape)
    return pl.BlockSpec(shape, lambda *_: zeros, pipeline_mode=pl.Buffered(1))


def _picked(shape, *lead):
    idx = tuple(lead) + (0,) * len(shape)
    return pl.BlockSpec((None,) * len(lead) + tuple(shape), lambda *_: idx, pipeline_mode=pl.Buffered(1))


def _ffn_body(x_ref, gpre_ref, gpost_ref, wg_ref, wu_ref, wd_ref, o_ref):
    x = x_ref[...]
    hb = _rms(x, gpre_ref[...]).astype(BF16)
    step = D_FF // FFN_CHUNKS
    y = None
    for c in range(FFN_CHUNKS):
        sl = slice(c * step, (c + 1) * step)
        g = jnp.dot(hb, wg_ref[:, sl], preferred_element_type=F32)
        u = jnp.dot(hb, wu_ref[:, sl], preferred_element_type=F32)
        a = (g / (1.0 + jnp.exp(-g)) * u).astype(BF16)
        part = jnp.dot(a, wd_ref[sl, :], preferred_element_type=F32)
        y = part if y is None else y + part
    o_ref[...] = x + 0.5 * _rms(y, gpost_ref[...])


def _ffn(x, gains, wg, wu, wd, layer, half):
    s = x.shape[0]
    row = pl.BlockSpec((FFN_ROWS, D_MODEL), lambda i: (i, 0))
    return pl.pallas_call(
        _ffn_body,
        grid=(s // FFN_ROWS,),
        in_specs=[row, _picked((1, D_MODEL), layer, 4 * half), _picked((1, D_MODEL), layer, 4 * half + 1),
                  _picked((D_MODEL, D_FF), layer, half), _picked((D_MODEL, D_FF), layer, half),
                  _picked((D_FF, D_MODEL), layer, half)],
        out_specs=row,
        out_shape=jax.ShapeDtypeStruct((s, D_MODEL), F32),
        compiler_params=_params("parallel"),
        name="ffn",
    )(x, gains, gains, wg, wu, wd)


N_NAT = 384 * 3 + 256 + 384
N_TR = 256 + 256 + 384 + 384


def _inproj_body(x_ref, g_ref, wn_ref, wt_ref, dil_ref, kb_ref, kc_ref, qbt_ref, vbt_ref, qct_ref, vct_ref,
                 kn_ref, run_ref):
    hb = _rms(x_ref[...], g_ref[...]).astype(BF16)
    pn = jnp.dot(hb, wn_ref[...], preferred_element_type=F32)
    dil_ref[:, 0:384] = pn[:, 0:384] * (HEAD_DIM ** -0.5 * LOG2E)
    dil_ref[:, 384:1152] = pn[:, 384:1152]
    kbf = pn[:, 1152:1408].astype(BF16)
    kb_ref[...] = kbf
    kc_ref[...] = pn[:, 1408:1792].astype(BF16)
    pt = lax.dot_general(wt_ref[...], hb, (((1,), (1,)), ((), ())), preferred_element_type=F32)
    qbt_ref[...] = (pt[0:256] * (DIFF_QK_DIM ** -0.5 * LOG2E)).astype(BF16)
    vbt_ref[...] = pt[256:512].astype(BF16)
    qct_ref[...] = (pt[512:896] * (HEAD_DIM ** -0.5 * LOG2E)).astype(BF16)
    vct_ref[...] = pt[896:1280].astype(BF16)

    @pl.when(pl.program_id(0) == 0)
    def _():
        run_ref[...] = jnp.zeros(run_ref.shape, F32)

    k2 = kbf.astype(F32)
    k2 = k2 * k2
    n2 = [jnp.sum(k2[:, p * LANES:(p + 1) * LANES], axis=1, keepdims=True) for p in range(N_HEADS_DIFF // 2)]
    lane = lax.broadcasted_iota(jnp.int32, (1, LANES), 1)
    rowi = lax.broadcasted_iota(jnp.int32, (SUBLANES, LANES), 0)
    run = run_ref[...]
    out = jnp.zeros((SUBLANES, LANES), F32)
    for r in range(PROJ_ROWS // ATT_TKV):
        tmax = [jnp.max(n[r * ATT_TKV:(r + 1) * ATT_TKV], axis=0, keepdims=True) for n in n2]
        run = jnp.maximum(run, jnp.where(lane == 0, tmax[0], jnp.where(lane == 1, tmax[1], 0.0)))
        out = jnp.where(rowi == r, jnp.sqrt(run), out)
    run_ref[...] = run
    kn_ref[0] = out


def _inproj(x, gains, wn, wt, layer):
    s = x.shape[0]
    t = PROJ_ROWS
    nat = lambda w: pl.BlockSpec((t, w), lambda i: (i, 0))
    tr = lambda w: pl.BlockSpec((w, t), lambda i: (0, i))
    return pl.pallas_call(
        _inproj_body,
        grid=(s // t,),
        in_specs=[nat(D_MODEL), _picked((1, D_MODEL), layer, 2), _picked((D_MODEL, N_NAT), layer),
                  _picked((N_TR, D_MODEL), layer)],
        out_specs=[nat(1152), nat(256), nat(384), tr(256), tr(256), tr(384), tr(384),
                   pl.BlockSpec((1, SUBLANES, LANES), lambda i: (i, 0, 0))],
        out_shape=[jax.ShapeDtypeStruct((s, 1152), F32),
                   jax.ShapeDtypeStruct((s, 256), BF16), jax.ShapeDtypeStruct((s, 384), BF16),
                   jax.ShapeDtypeStruct((256, s), BF16), jax.ShapeDtypeStruct((256, s), BF16),
                   jax.ShapeDtypeStruct((384, s), BF16), jax.ShapeDtypeStruct((384, s), BF16),
                   jax.ShapeDtypeStruct((s // t, SUBLANES, LANES), F32)],
        scratch_shapes=[pltpu.VMEM((1, LANES), F32)],
        compiler_params=_params("arbitrary"),
        name="inproj",
    )(x, gains, wn, wt)


def _dil_body(slope_ref, q_ref, kp_ref, kc_ref, vp_ref, vc_ref, o_ref, kbuf, vbuf, oacc, macc, lacc):
    n = pl.program_id(1)
    kbuf[0:DIL_CHUNK, :] = kp_ref[...]
    kbuf[DIL_CHUNK:, :] = kc_ref[...]
    vbuf[0:DIL_CHUNK, :] = vp_ref[...]
    vbuf[DIL_CHUNK:, :] = vc_ref[...]
    slope = slope_ref[0] * LOG2E
    lane = lax.broadcasted_iota(jnp.int32, (1, LANES), 1)
    head0 = lane < HEAD_DIM
    slope0 = jnp.max(jnp.where(head0, slope, 0.0), axis=1, keepdims=True)
    slope1 = jnp.max(jnp.where(head0, 0.0, slope), axis=1, keepdims=True)
    qi = lax.broadcasted_iota(jnp.int32, (N_BACK, 2 * N_BACK), 0)
    ki = lax.broadcasted_iota(jnp.int32, (N_BACK, 2 * N_BACK), 1)
    dist = N_BACK + qi - ki
    band = (dist >= 0) & (dist <= N_BACK)
    distf = dist.astype(F32)
    nt = (((1,), (1,)), ((), ()))

    for bi, (window, d) in enumerate(DILATED_PAIRS):
        per_chunk = DIL_CHUNK // window
        biases = [jnp.where(band, -(sl * float(d)) * distf, -jnp.inf) for sl in (slope0, slope1)]

        def tile(it, carry, d=d, window=window, per_chunk=per_chunk, bi=bi, biases=biases):
            if d == 1:
                c, r = it, 0
            elif per_chunk == 1:
                c, r = 0, it
            else:
                c, r = it // d, it % d
            q_start = c * window + r
            k_start = DIL_CHUNK + q_start - window
            if d == 1:
                rows_q = pl.ds(q_start, N_BACK)
                rows_k = pl.ds(k_start, 2 * N_BACK)
            else:
                rows_q = pl.ds(q_start, N_BACK, stride=d)
                rows_k = pl.ds(k_start, 2 * N_BACK, stride=d)
            q2 = q_ref[rows_q, :]
            k2 = kbuf[rows_k, :].astype(BF16)
            v2 = vbuf[rows_k, :].astype(BF16)
            first = jnp.logical_and(n == 0, c == 0)
            outs, ms, ls = [], [], []
            for hm, bias in zip((head0, jnp.logical_not(head0)), biases):
                qe = jnp.where(hm, q2, 0.0).astype(BF16)
                sc = lax.dot_general(qe, k2, nt, preferred_element_type=F32) + bias
                sc = jnp.concatenate([jnp.where(first, -jnp.inf, sc[:, :N_BACK]), sc[:, N_BACK:]], axis=1)
                m = jnp.max(sc, axis=1, keepdims=True)
                p = jnp.exp2(sc - m)
                ls.append(jnp.sum(p, axis=1, keepdims=True))
                ms.append(m)
                outs.append(jnp.dot(p.astype(BF16), v2, preferred_element_type=F32))
            o_blk = jnp.where(head0, outs[0], outs[1])
            m_blk = jnp.where(head0, ms[0], ms[1])
            l_blk = jnp.where(head0, ls[0], ls[1])
            if bi == 0:
                oacc[rows_q, :] = o_blk
                macc[rows_q, :] = m_blk
                lacc[rows_q, :] = l_blk
            else:
                m_old = macc[rows_q, :]
                m_new = jnp.maximum(m_old, m_blk)
                a = jnp.exp2(m_old - m_new)
                b = jnp.exp2(m_blk - m_new)
                oacc[rows_q, :] = a * oacc[rows_q, :] + b * o_blk
                lacc[rows_q, :] = a * lacc[rows_q, :] + b * l_blk
                macc[rows_q, :] = m_new
            return carry

        lax.fori_loop(0, DIL_CHUNK // N_BACK, tile, 0, unroll=DIL_UNROLL)

    o_ref[...] = (oacc[...] / lacc[...]).astype(BF16)


def _dilated(dil, slopes):
    s = dil.shape[0]
    npairs = N_HEADS_DIL // 2
    blk = lambda col0, prev: pl.BlockSpec(
        (DIL_CHUNK, LANES),
        (lambda p, n: (jnp.maximum(n - 1, 0), col0 + p)) if prev else (lambda p, n: (n, col0 + p)))
    buf = lambda rows: pltpu.VMEM((rows, LANES), F32)
    return pl.pallas_call(
        _dil_body,
        grid=(npairs, s // DIL_CHUNK),
        in_specs=[pl.BlockSpec((1, 1, LANES), lambda p, n: (p, 0, 0)),
                  blk(0, False), blk(3, True), blk(3, False), blk(6, True), blk(6, False)],
        out_specs=pl.BlockSpec((DIL_CHUNK, LANES), lambda p, n: (n, p)),
        out_shape=jax.ShapeDtypeStruct((s, N_HEADS_DIL * HEAD_DIM), BF16),
        scratch_shapes=[buf(2 * DIL_CHUNK), buf(2 * DIL_CHUNK), buf(DIL_CHUNK), buf(DIL_CHUNK), buf(DIL_CHUNK)],
        compiler_params=_params("parallel", "parallel"),
        name="dilated",
    )(slopes, dil, dil, dil, dil, dil)


def _diff_body(lam_init, kn_ref, slope_ref, lam_ref, gain_ref, qt_ref, k_ref, vt_ref, o_ref,
               m_ref, acc_ref, s0_ref, s1_ref):
    tq, tkv = ATT_TQ, ATT_TKV
    pair = pl.program_id(0)
    i = pl.program_id(1)
    qt = qt_ref[...]
    row = lax.broadcasted_iota(jnp.int32, (LANES, tq), 0)
    zero = jnp.zeros_like(qt)
    top = jnp.concatenate(
        [jnp.where((row >= DIFF_QK_DIM * c) & (row < DIFF_QK_DIM * (c + 1)), qt, zero) for c in range(4)], axis=1)
    sl2 = slope_ref[0] * LOG2E
    sl_hi = sl2.astype(BF16).astype(F32)
    sl_lo = sl2 - sl_hi
    brow = lax.broadcasted_iota(jnp.int32, (LANES, 4 * tq), 0)
    bottom = jnp.where(brow == 0, sl_hi, jnp.where(brow == 1, sl_lo, 0.0)).astype(BF16)
    qa = jnp.concatenate([top, bottom], axis=0)
    prow = lax.broadcasted_iota(jnp.int32, (tkv, LANES), 0)
    pcol = lax.broadcasted_iota(jnp.int32, (tkv, LANES), 1)
    kofs = jnp.where(pcol < 2, prow, 0).astype(F32).astype(BF16)
    topf = top.astype(F32)
    qn = jnp.sqrt(jnp.sum(topf * topf, axis=0, keepdims=True))
    kpos = lax.broadcasted_iota(jnp.int32, (tkv, 4 * tq), 0)
    qpos = lax.broadcasted_iota(jnp.int32, (tkv, 4 * tq), 1) & (tq - 1)
    causal = kpos <= qpos

    m_ref[...] = jnp.full(m_ref.shape, NEG_BIG, F32)
    acc_ref[...] = jnp.zeros(acc_ref.shape, F32)
    ones = jnp.ones((DIFF_ACC_ROWS - HEAD_DIM, tkv), BF16)

    def tile_start(j):
        return pl.multiple_of(jnp.maximum(j, 0) * tkv, tkv)

    def lanes(heads):
        return slice(2 * tq * heads[0], 2 * tq * (heads[-1] + 1))

    def scores(j, heads):
        k = k_ref[pl.ds(tile_start(j), tkv), :]
        return jnp.dot(jnp.concatenate([k, kofs], axis=1), qa[:, lanes(heads)], preferred_element_type=F32)

    def softmax_step(s, j, masked, heads):
        ln = lanes(heads)
        if masked:
            s = jnp.where(causal[:, ln], s, -jnp.inf)
        c = jnp.where(j >= 0, sl2[:, ln] * (j * tkv - i * tq).astype(F32), NEG_BIG)
        m_old = m_ref[:, ln]
        m_new = jnp.maximum(m_old, jnp.max(s, axis=0, keepdims=True) + c)
        pb = jnp.exp2(s - (m_new - c)).astype(BF16)
        alpha = jnp.exp2(m_old - m_new)
        m_ref[:, ln] = m_new
        vt = vt_ref[:, pl.ds(tile_start(j), tkv)]
        h, r = HEAD_DIM, DIFF_ACC_ROWS
        for n, e in enumerate(heads):
            ve = jnp.concatenate([vt[e * h:(e + 1) * h], ones], axis=0)
            cols = slice(2 * tq * n, 2 * tq * (n + 1))
            acc_ref[e * r:(e + 1) * r, :] = alpha[:, cols] * acc_ref[e * r:(e + 1) * r, :] + jnp.dot(
                ve, pb[:, cols], preferred_element_type=F32)

    def alive(j, heads):
        ln = lanes(heads)
        kn = kn_ref[pair, jnp.maximum(j, 0)]
        reach = qn[:, ln] * kn + sl2[:, ln] * ((j * tkv - i * tq).astype(F32) + (tkv - 1.0)) - m_ref[:, ln]
        return jnp.max(reach) > DEAD_EXP2

    def sweep(j_first, heads, watch):
        ln = lanes(heads)
        bufs = (s0_ref, s1_ref)

        def more(state):
            u, live = state
            return jnp.logical_and(DIFF_GROUP * u <= j_first, live)

        def tile_group(state):
            u, _ = state
            ja = j_first - DIFF_GROUP * u
            live = alive(ja - DIFF_GROUP, watch)
            for t in range(DIFF_GROUP):
                bufs[(t + 1) % 2][:, ln] = scores(ja - t - 1, heads)
                softmax_step(bufs[t % 2][:, ln], ja - t, False, heads)
            return u + 1, live

        assert DIFF_GROUP % 2 == 0
        groups, _ = lax.while_loop(more, tile_group, (0, alive(j_first, watch)))
        return j_first - DIFF_GROUP * groups

    both = (0, 1)
    s_diag = scores(i, both)
    s0_ref[...] = scores(i - 1, both)
    softmax_step(s_diag, i, True, both)
    j_rest = sweep(i - 1, both, (0,))
    sweep(j_rest, (1,), (1,))

    lp = lam_ref[...]
    lam = (jnp.exp(jnp.sum(lp[0:1] * lp[1:2], axis=1, keepdims=True))
           - jnp.exp(jnp.sum(lp[2:3] * lp[3:4], axis=1, keepdims=True)) + lam_init)
    acc = acc_ref[...]
    outs = []
    for e in range(2):
        a = acc[e * DIFF_ACC_ROWS:e * DIFF_ACC_ROWS + HEAD_DIM]
        inv = 1.0 / acc[e * DIFF_ACC_ROWS + HEAD_DIM:e * DIFF_ACC_ROWS + HEAD_DIM + 1]
        i1 = inv[:, 0:tq]
        i2 = inv[:, tq:]
        o = a[:, 0:tq] * i1 - lam * (a[:, tq:] * i2)
        o = o * lax.rsqrt(jnp.mean(o * o, axis=0, keepdims=True) + NORM_EPS)
        outs.append(o * gain_ref[...] * (1.0 - lam_init))
    o_ref[...] = jnp.concatenate(outs, axis=0).T.astype(BF16)


def _diff_attention(qbt, kb, vbt, knorm, lam_params, gain_col, slopes, layer, lam_init):
    s = kb.shape[0]
    tq = ATT_TQ
    npairs = N_HEADS_DIFF // 2
    return pl.pallas_call(
        functools.partial(_diff_body, lam_init),
        grid=(npairs, s // tq),
        in_specs=[pl.BlockSpec(memory_space=pltpu.SMEM),
                  pl.BlockSpec((1, 1, 4 * tq), lambda p, i: (p, 0, 0)),
                  pl.BlockSpec((None, 4, DIFF_QK_DIM), lambda p, i: (layer, 0, 0)),
                  pl.BlockSpec((None, HEAD_DIM, 1), lambda p, i: (layer, 0, 0)),
                  pl.BlockSpec((LANES, tq), lambda p, i: (p, i)),
                  pl.BlockSpec((s, LANES), lambda p, i: (0, p)),
                  pl.BlockSpec((LANES, s), lambda p, i: (p, 0))],
        out_specs=pl.BlockSpec((tq, LANES), lambda p, i: (i, p)),
        out_shape=jax.ShapeDtypeStruct((s, N_HEADS_DIFF * HEAD_DIM), BF16),
        scratch_shapes=[pltpu.VMEM((1, 4 * tq), F32), pltpu.VMEM((2 * DIFF_ACC_ROWS, 2 * tq), F32),
                        pltpu.VMEM((ATT_TKV, 4 * tq), F32), pltpu.VMEM((ATT_TKV, 4 * tq), F32)],
        compiler_params=_params("parallel", "arbitrary"),
        name="diff_attn",
    )(knorm, slopes, lam_params, gain_col, qbt, kb, vbt)


def _sb_body(qt_ref, k_ref, vt_ref, o_ref, carry_ref, acc_ref):
    tq, tkv = ATT_TQ, ATT_TKV
    step = pl.program_id(1)
    row = lax.broadcasted_iota(jnp.int32, (LANES, tq), 0)
    ur = lax.broadcasted_iota(jnp.int32, (tkv, tkv), 0)
    uc = lax.broadcasted_iota(jnp.int32, (tkv, tkv), 1)
    upper = (uc >= ur).astype(BF16)
    kpos = lax.broadcasted_iota(jnp.int32, (tkv, 2 * tq), 0)
    qpos = lax.broadcasted_iota(jnp.int32, (tkv, 2 * tq), 1) & (tq - 1)
    causal = kpos < qpos

    carry_ref[...] = jnp.zeros(carry_ref.shape, F32)
    acc_ref[...] = jnp.zeros(acc_ref.shape, F32)
    qas = []
    for s in range(SB_SUB):
        qt = qt_ref[:, s * tq:(s + 1) * tq]
        zero = jnp.zeros_like(qt)
        qas.append(jnp.concatenate([jnp.where(row < HEAD_DIM, qt, zero), jnp.where(row >= HEAD_DIM, qt, zero)], axis=1))

    def stages(s, j, masked, guarded):
        v = {}

        def logits():
            v["ks"] = pl.multiple_of(jnp.maximum(j, 0) * tkv, tkv)
            z = jnp.dot(k_ref[pl.ds(v["ks"], tkv), :], qas[s], preferred_element_type=F32)
            v["z"] = jnp.where(j >= 0, z, NEG_BIG) if guarded else z

        def softplus():
            z = v["z"]
            sp = jnp.maximum(z, 0.0) + jnp.log(1.0 + jnp.exp2(-jnp.abs(z))) * LOG2E
            if masked:
                sp = jnp.where(causal, sp, 0.0)
            v["hi"] = sp.astype(BF16)
            v["lo"] = (sp - v["hi"].astype(F32)).astype(BF16)

        def suffix_sum():
            v["cs"] = (jnp.dot(upper, v["hi"], preferred_element_type=F32)
                       + jnp.dot(upper, v["lo"], preferred_element_type=F32))

        def weights():
            w = jnp.exp2((v["z"] + carry_ref[s]) - v["cs"])
            if masked:
                w = jnp.where(causal, w, 0.0)
            v["wb"] = w.astype(BF16)
            carry_ref[s] = carry_ref[s] - v["cs"][0:1]

        def values():
            vt = vt_ref[:, pl.ds(v["ks"], tkv)]
            h, wb = HEAD_DIM, v["wb"]
            acc_ref[s, 0:h, :] = acc_ref[s, 0:h, :] + jnp.dot(vt[0:h], wb[:, 0:tq], preferred_element_type=F32)
            acc_ref[s, h:, :] = acc_ref[s, h:, :] + jnp.dot(vt[h:], wb[:, tq:], preferred_element_type=F32)

        return (logits, softplus, suffix_sum, weights, values)

    chains = ([stages(s, SB_SUB * step + s, True, False) for s in range(SB_SUB)]
              + [stages(s, SB_SUB * step + s - 1, False, True) for s in range(SB_SUB)])
    n_stage = len(chains[0])
    for t in range(len(chains) + n_stage - 1):
        for c, chain in enumerate(chains):
            if 0 <= t - c < n_stage:
                chain[t - c]()

    for s in range(SB_SUB):
        i = SB_SUB * step + s

        def alive(s=s):
            return jnp.max(carry_ref[s]) > DEAD_EXP2

        def more(state, i=i):
            it, live = state
            return jnp.logical_and(it < i, live)

        def full_tile(state, s=s, i=i, alive=alive):
            it, _ = state
            for stage in stages(s, i - 1 - it, False, False):
                stage()
            return it + 1, alive()

        lax.while_loop(more, full_tile, (1, alive()))
        o_ref[s * tq:(s + 1) * tq, :] = acc_ref[s].T.astype(BF16)


def _sb_attention(qct, kc, vct):
    s = kc.shape[0]
    tq = ATT_TQ * SB_SUB
    npairs = N_HEADS_SB // 2
    return pl.pallas_call(
        _sb_body,
        grid=(npairs, s // tq),
        in_specs=[pl.BlockSpec((LANES, tq), lambda p, i: (p, i)),
                  pl.BlockSpec((s, LANES), lambda p, i: (0, p)),
                  pl.BlockSpec((LANES, s), lambda p, i: (p, 0))],
        out_specs=pl.BlockSpec((tq, LANES), lambda p, i: (i, p)),
        out_shape=jax.ShapeDtypeStruct((s, N_HEADS_SB * HEAD_DIM), BF16),
        scratch_shapes=[pltpu.VMEM((SB_SUB, 1, 2 * ATT_TQ), F32), pltpu.VMEM((SB_SUB, LANES, ATT_TQ), F32)],
        compiler_params=_params("parallel", "arbitrary"),
        name="sb_attn",
    )(qct, kc, vct)


def _outproj_body(x_ref, g_ref, oa_ref, ob_ref, oc_ref, w_ref, o_ref):
    mixed = jnp.concatenate([oa_ref[...], ob_ref[...], oc_ref[...]], axis=1)
    y = jnp.dot(mixed, w_ref[...], preferred_element_type=F32)
    o_ref[...] = x_ref[...] + _rms(y, g_ref[...])


def _outproj(x, gains, oa, ob, oc, wo, layer):
    s = x.shape[0]
    t = PROJ_ROWS
    nat = lambda w: pl.BlockSpec((t, w), lambda i: (i, 0))
    return pl.pallas_call(
        _outproj_body,
        grid=(s // t,),
        in_specs=[nat(D_MODEL), _picked((1, D_MODEL), layer, 3), nat(384), nat(256), nat(384),
                  _picked((D_MODEL, D_MODEL), layer)],
        out_specs=nat(D_MODEL),
        out_shape=jax.ShapeDtypeStruct((s, D_MODEL), F32),
        compiler_params=_params("parallel"),
        name="outproj",
    )(x, gains, oa, ob, oc, wo)


def _alibi_slopes(n):
    return 2.0 ** (-8.0 * jnp.arange(1, n + 1, dtype=F32) / n)


def kernel(x, norm_gains, w_ffn_gate, w_ffn_up, w_ffn_down, w_in, w_out, diff_lambda, diff_subln_gain):
    b, s, d = x.shape
    assert b == 1 and d == D_MODEL and s % DIL_CHUNK == 0 and ATT_TQ == ATT_TKV
    depth = norm_gains.shape[0]
    xs = x.reshape(s, d)

    off = [0]
    for w in IN_SIZES:
        off.append(off[-1] + w)
    col = lambda i: w_in[:, :, off[i]:off[i + 1]]

    sl_dil = jnp.repeat(_alibi_slopes(N_HEADS_DIL), HEAD_DIM).reshape(N_HEADS_DIL // 2, 1, LANES)
    sl_diff = jnp.repeat(_alibi_slopes(N_HEADS_DIFF), 2 * ATT_TQ).reshape(N_HEADS_DIFF // 2, 1, 4 * ATT_TQ)

    gains = norm_gains.reshape(depth, -1, 1, D_MODEL)
    wg = w_ffn_gate.astype(BF16)
    wu = w_ffn_up.astype(BF16)
    wd = w_ffn_down.astype(BF16)
    wn = jnp.concatenate([col(0), col(1), col(2), col(4), col(7)], axis=2).astype(BF16)
    wt = jnp.concatenate([col(3), col(5), col(6), col(8)], axis=2).transpose(0, 2, 1).astype(BF16)
    wo = w_out.astype(BF16)
    sub_gain = diff_subln_gain.reshape(depth, HEAD_DIM, 1)
    tiles_per_step = PROJ_ROWS // ATT_TKV
    assert PROJ_ROWS % ATT_TKV == 0 and tiles_per_step <= SUBLANES

    for layer in range(depth):
        lam_init = 0.8 - 0.6 * math.exp(-0.3 * layer)
        xs = _ffn(xs, gains, wg, wu, wd, layer, 0)
        dil, kb, kc, qbt, vbt, qct, vct, kn = _inproj(xs, gains, wn, wt, layer)
        knorm = kn[:, :tiles_per_step, :N_HEADS_DIFF // 2].reshape(s // ATT_TKV, N_HEADS_DIFF // 2).T
        oa = _dilated(dil, sl_dil)
        ob = _diff_attention(qbt, kb, vbt, knorm, diff_lambda, sub_gain, sl_diff, layer, lam_init)
        oc = _sb_attention(qct, kc, vct)
        xs = _outproj(xs, gains, oa, ob, oc, wo, layer)
        xs = _ffn(xs, gains, wg, wu, wd, layer, 1)
    return xs.reshape(b, s, d)
```

```python
import functools
import math

import jax
import jax.numpy as jnp
from jax import lax
from jax.experimental import pallas as pl
from jax.experimental.pallas import tpu as pltpu

D_MODEL = 1024
D_FF = 2816
HEAD_DIM = 64
N_HEADS_DIL = 6
N_HEADS_DIFF = 4
N_HEADS_SB = 6
DIFF_QK_DIM = HEAD_DIM // 2
DILATED_PAIRS = ((128, 1), (512, 4), (2048, 16))
N_BACK = 128
NORM_EPS = 1e-6
IN_SIZES = (384, 384, 384, 256, 256, 256, 384, 384, 384)

LANES = 128
SUBLANES = 8
VMEM_LIMIT = 56 * 1024 * 1024

FFN_ROWS = 512
FFN_CHUNKS = 11
PROJ_ROWS = 512
DIL_CHUNK = 2048
DIL_UNROLL = 16
ATT_TQ = 256
ATT_TKV = 256
BF16_ROWS = 16
DIFF_ACC_ROWS = HEAD_DIM + BF16_ROWS
DIFF_GROUP = 4
DIFF_SUB = 4
SB_SUB = 8
NEG_BIG = -1e30
LOG2E = 1.4426950408889634
DEAD_EXP2 = -160.0

F32 = jnp.float32
BF16 = jnp.bfloat16


def _rms(x, g):
    return x * lax.rsqrt(jnp.mean(x * x, axis=-1, keepdims=True) + NORM_EPS) * g


def _params(*sem):
    return pltpu.CompilerParams(dimension_semantics=sem, vmem_limit_bytes=VMEM_LIMIT)


def _resident(shape):
    zeros = (0,) * len(shape)
    return pl.BlockSpec(shape, lambda *_: zeros, pipeline_mode=pl.Buffered(1))


def _picked(shape, *lead):
    idx = tuple(lead) + (0,) * len(shape)
    return pl.BlockSpec((None,) * len(lead) + tuple(shape), lambda *_: idx, pipeline_mode=pl.Buffered(1))


def _ffn_body(x_ref, gpre_ref, gpost_ref, wg_ref, wu_ref, wd_ref, o_ref):
    x = x_ref[...]
    hb = _rms(x, gpre_ref[...]).astype(BF16)
    step = D_FF // FFN_CHUNKS
    y = None
    for c in range(FFN_CHUNKS):
        sl = slice(c * step, (c + 1) * step)
        g = jnp.dot(hb, wg_ref[:, sl], preferred_element_type=F32)
        u = jnp.dot(hb, wu_ref[:, sl], preferred_element_type=F32)
        a = (g / (1.0 + jnp.exp(-g)) * u).astype(BF16)
        part = jnp.dot(a, wd_ref[sl, :], preferred_element_type=F32)
        y = part if y is None else y + part
    o_ref[...] = x + 0.5 * _rms(y, gpost_ref[...])


def _ffn(x, gains, wg, wu, wd, layer, half):
    s = x.shape[0]
    row = pl.BlockSpec((FFN_ROWS, D_MODEL), lambda i: (i, 0))
    return pl.pallas_call(
        _ffn_body,
        grid=(s // FFN_ROWS,),
        in_specs=[row, _picked((1, D_MODEL), layer, 4 * half), _picked((1, D_MODEL), layer, 4 * half + 1),
                  _picked((D_MODEL, D_FF), layer, half), _picked((D_MODEL, D_FF), layer, half),
                  _picked((D_FF, D_MODEL), layer, half)],
        out_specs=row,
        out_shape=jax.ShapeDtypeStruct((s, D_MODEL), F32),
        compiler_params=_params("parallel"),
        name="ffn",
    )(x, gains, gains, wg, wu, wd)


N_NAT = 384 * 3 + 256 + 384
N_TR = 256 + 256 + 384 + 384


def _inproj_body(x_ref, g_ref, wn_ref, wt_ref, dil_ref, kb_ref, kc_ref, qbt_ref, vbt_ref, qct_ref, vct_ref,
                 kn_ref, run_ref):
    hb = _rms(x_ref[...], g_ref[...]).astype(BF16)
    pn = jnp.dot(hb, wn_ref[...], preferred_element_type=F32)
    dil_ref[:, 0:384] = pn[:, 0:384] * (HEAD_DIM ** -0.5 * LOG2E)
    dil_ref[:, 384:1152] = pn[:, 384:1152]
    kbf = pn[:, 1152:1408].astype(BF16)
    kb_ref[...] = kbf
    kc_ref[...] = pn[:, 1408:1792].astype(BF16)
    pt = lax.dot_general(wt_ref[...], hb, (((1,), (1,)), ((), ())), preferred_element_type=F32)
    qbt_ref[...] = (pt[0:256] * (DIFF_QK_DIM ** -0.5 * LOG2E)).astype(BF16)
    vbt_ref[...] = pt[256:512].astype(BF16)
    qct_ref[...] = (pt[512:896] * (HEAD_DIM ** -0.5 * LOG2E)).astype(BF16)
    vct_ref[...] = pt[896:1280].astype(BF16)

    @pl.when(pl.program_id(0) == 0)
    def _():
        run_ref[...] = jnp.zeros(run_ref.shape, F32)

    k2 = kbf.astype(F32)
    k2 = k2 * k2
    n2 = [jnp.sum(k2[:, p * LANES:(p + 1) * LANES], axis=1, keepdims=True) for p in range(N_HEADS_DIFF // 2)]
    lane = lax.broadcasted_iota(jnp.int32, (1, LANES), 1)
    rowi = lax.broadcasted_iota(jnp.int32, (SUBLANES, LANES), 0)
    run = run_ref[...]
    out = jnp.zeros((SUBLANES, LANES), F32)
    for r in range(PROJ_ROWS // ATT_TKV):
        tmax = [jnp.max(n[r * ATT_TKV:(r + 1) * ATT_TKV], axis=0, keepdims=True) for n in n2]
        run = jnp.maximum(run, jnp.where(lane == 0, tmax[0], jnp.where(lane == 1, tmax[1], 0.0)))
        out = jnp.where(rowi == r, jnp.sqrt(run), out)
    run_ref[...] = run
    kn_ref[0] = out


def _inproj(x, gains, wn, wt, layer):
    s = x.shape[0]
    t = PROJ_ROWS
    nat = lambda w: pl.BlockSpec((t, w), lambda i: (i, 0))
    tr = lambda w: pl.BlockSpec((w, t), lambda i: (0, i))
    return pl.pallas_call(
        _inproj_body,
        grid=(s // t,),
        in_specs=[nat(D_MODEL), _picked((1, D_MODEL), layer, 2), _picked((D_MODEL, N_NAT), layer),
                  _picked((N_TR, D_MODEL), layer)],
        out_specs=[nat(1152), nat(256), nat(384), tr(256), tr(256), tr(384), tr(384),
                   pl.BlockSpec((1, SUBLANES, LANES), lambda i: (i, 0, 0))],
        out_shape=[jax.ShapeDtypeStruct((s, 1152), F32),
                   jax.ShapeDtypeStruct((s, 256), BF16), jax.ShapeDtypeStruct((s, 384), BF16),
                   jax.ShapeDtypeStruct((256, s), BF16), jax.ShapeDtypeStruct((256, s), BF16),
                   jax.ShapeDtypeStruct((384, s), BF16), jax.ShapeDtypeStruct((384, s), BF16),
                   jax.ShapeDtypeStruct((s // t, SUBLANES, LANES), F32)],
        scratch_shapes=[pltpu.VMEM((1, LANES), F32)],
        compiler_params=_params("arbitrary"),
        name="inproj",
    )(x, gains, wn, wt)


def _dil_body(slope_ref, q_ref, kp_ref, kc_ref, vp_ref, vc_ref, o_ref, kbuf, vbuf, oacc, macc, lacc):
    n = pl.program_id(1)
    kbuf[0:DIL_CHUNK, :] = kp_ref[...]
    kbuf[DIL_CHUNK:, :] = kc_ref[...]
    vbuf[0:DIL_CHUNK, :] = vp_ref[...]
    vbuf[DIL_CHUNK:, :] = vc_ref[...]
    slope = slope_ref[0] * LOG2E
    lane = lax.broadcasted_iota(jnp.int32, (1, LANES), 1)
    head0 = lane < HEAD_DIM
    slope0 = jnp.max(jnp.where(head0, slope, 0.0), axis=1, keepdims=True)
    slope1 = jnp.max(jnp.where(head0, 0.0, slope), axis=1, keepdims=True)
    qi = lax.broadcasted_iota(jnp.int32, (N_BACK, 2 * N_BACK), 0)
    ki = lax.broadcasted_iota(jnp.int32, (N_BACK, 2 * N_BACK), 1)
    dist = N_BACK + qi - ki
    band = (dist >= 0) & (dist <= N_BACK)
    distf = dist.astype(F32)
    nt = (((1,), (1,)), ((), ()))

    for bi, (window, d) in enumerate(DILATED_PAIRS):
        per_chunk = DIL_CHUNK // window
        biases = [jnp.where(band, -(sl * float(d)) * distf, -jnp.inf) for sl in (slope0, slope1)]

        def tile(it, carry, d=d, window=window, per_chunk=per_chunk, bi=bi, biases=biases):
            if d == 1:
                c, r = it, 0
            elif per_chunk == 1:
                c, r = 0, it
            else:
                c, r = it // d, it % d
            q_start = c * window + r
            k_start = DIL_CHUNK + q_start - window
            if d == 1:
                rows_q = pl.ds(q_start, N_BACK)
                rows_k = pl.ds(k_start, 2 * N_BACK)
            else:
                rows_q = pl.ds(q_start, N_BACK, stride=d)
                rows_k = pl.ds(k_start, 2 * N_BACK, stride=d)
            q2 = q_ref[rows_q, :]
            k2 = kbuf[rows_k, :].astype(BF16)
            v2 = vbuf[rows_k, :].astype(BF16)
            first = jnp.logical_and(n == 0, c == 0)
            outs, ms, ls = [], [], []
            for hm, bias in zip((head0, jnp.logical_not(head0)), biases):
                qe = jnp.where(hm, q2, 0.0).astype(BF16)
                sc = lax.dot_general(qe, k2, nt, preferred_element_type=F32) + bias
                sc = jnp.concatenate([jnp.where(first, -jnp.inf, sc[:, :N_BACK]), sc[:, N_BACK:]], axis=1)
                m = jnp.max(sc, axis=1, keepdims=True)
                p = jnp.exp2(sc - m)
                ls.append(jnp.sum(p, axis=1, keepdims=True))
                ms.append(m)
                outs.append(jnp.dot(p.astype(BF16), v2, preferred_element_type=F32))
            o_blk = jnp.where(head0, outs[0], outs[1])
            m_blk = jnp.where(head0, ms[0], ms[1])
            l_blk = jnp.where(head0, ls[0], ls[1])
            if bi == 0:
                oacc[rows_q, :] = o_blk
                macc[rows_q, :] = m_blk
                lacc[rows_q, :] = l_blk
            else:
                m_old = macc[rows_q, :]
                m_new = jnp.maximum(m_old, m_blk)
                a = jnp.exp2(m_old - m_new)
                b = jnp.exp2(m_blk - m_new)
                oacc[rows_q, :] = a * oacc[rows_q, :] + b * o_blk
                lacc[rows_q, :] = a * lacc[rows_q, :] + b * l_blk
                macc[rows_q, :] = m_new
            return carry

        lax.fori_loop(0, DIL_CHUNK // N_BACK, tile, 0, unroll=DIL_UNROLL)

    o_ref[...] = (oacc[...] / lacc[...]).astype(BF16)


def _dilated(dil, slopes):
    s = dil.shape[0]
    npairs = N_HEADS_DIL // 2
    blk = lambda col0, prev: pl.BlockSpec(
        (DIL_CHUNK, LANES),
        (lambda p, n: (jnp.maximum(n - 1, 0), col0 + p)) if prev else (lambda p, n: (n, col0 + p)))
    buf = lambda rows: pltpu.VMEM((rows, LANES), F32)
    return pl.pallas_call(
        _dil_body,
        grid=(npairs, s // DIL_CHUNK),
        in_specs=[pl.BlockSpec((1, 1, LANES), lambda p, n: (p, 0, 0)),
                  blk(0, False), blk(3, True), blk(3, False), blk(6, True), blk(6, False)],
        out_specs=pl.BlockSpec((DIL_CHUNK, LANES), lambda p, n: (n, p)),
        out_shape=jax.ShapeDtypeStruct((s, N_HEADS_DIL * HEAD_DIM), BF16),
        scratch_shapes=[buf(2 * DIL_CHUNK), buf(2 * DIL_CHUNK), buf(DIL_CHUNK), buf(DIL_CHUNK), buf(DIL_CHUNK)],
        compiler_params=_params("parallel", "parallel"),
        name="dilated",
    )(slopes, dil, dil, dil, dil, dil)


def _diff_body(lam_init, kn_ref, slope_ref, lam_ref, gain_ref, qt_ref, k_ref, vt_ref, o_ref, *scratch):
    tq = ATT_TQ
    for sub in range(DIFF_SUB):
        _diff_query_tile(lam_init, DIFF_SUB * pl.program_id(1) + sub, kn_ref, slope_ref, lam_ref, gain_ref,
                         qt_ref.at[:, sub * tq:(sub + 1) * tq], k_ref, vt_ref,
                         o_ref.at[sub * tq:(sub + 1) * tq, :], *scratch)


def _diff_query_tile(lam_init, i, kn_ref, slope_ref, lam_ref, gain_ref, qt_ref, k_ref, vt_ref, o_ref,
                     m_ref, acc_ref, s0_ref, s1_ref):
    tq, tkv = ATT_TQ, ATT_TKV
    pair = pl.program_id(0)
    qt = qt_ref[...]
    row = lax.broadcasted_iota(jnp.int32, (LANES, tq), 0)
    zero = jnp.zeros_like(qt)
    top = jnp.concatenate(
        [jnp.where((row >= DIFF_QK_DIM * c) & (row < DIFF_QK_DIM * (c + 1)), qt, zero) for c in range(4)], axis=1)
    sl2 = slope_ref[0] * LOG2E
    sl_hi = sl2.astype(BF16).astype(F32)
    sl_lo = sl2 - sl_hi
    brow = lax.broadcasted_iota(jnp.int32, (LANES, 4 * tq), 0)
    bottom = jnp.where(brow == 0, sl_hi, jnp.where(brow == 1, sl_lo, 0.0)).astype(BF16)
    qa = jnp.concatenate([top, bottom], axis=0)
    prow = lax.broadcasted_iota(jnp.int32, (tkv, LANES), 0)
    pcol = lax.broadcasted_iota(jnp.int32, (tkv, LANES), 1)
    kofs = jnp.where(pcol < 2, prow, 0).astype(F32).astype(BF16)
    topf = top.astype(F32)
    qn = jnp.sqrt(jnp.sum(topf * topf, axis=0, keepdims=True))
    kpos = lax.broadcasted_iota(jnp.int32, (tkv, 4 * tq), 0)
    qpos = lax.broadcasted_iota(jnp.int32, (tkv, 4 * tq), 1) & (tq - 1)
    causal = kpos <= qpos

    m_ref[...] = jnp.full(m_ref.shape, NEG_BIG, F32)
    acc_ref[...] = jnp.zeros(acc_ref.shape, F32)
    ones = jnp.ones((DIFF_ACC_ROWS - HEAD_DIM, tkv), BF16)

    def tile_start(j):
        return pl.multiple_of(jnp.maximum(j, 0) * tkv, tkv)

    def lanes(heads):
        return slice(2 * tq * heads[0], 2 * tq * (heads[-1] + 1))

    def scores(j, heads):
        k = k_ref[pl.ds(tile_start(j), tkv), :]
        return jnp.dot(jnp.concatenate([k, kofs], axis=1), qa[:, lanes(heads)], preferred_element_type=F32)

    def softmax_step(s, j, masked, heads):
        ln = lanes(heads)
        if masked:
            s = jnp.where(causal[:, ln], s, -jnp.inf)
        c = jnp.where(j >= 0, sl2[:, ln] * (j * tkv - i * tq).astype(F32), NEG_BIG)
        m_old = m_ref[:, ln]
        m_new = jnp.maximum(m_old, jnp.max(s, axis=0, keepdims=True) + c)
        pb = jnp.exp2(s - (m_new - c)).astype(BF16)
        alpha = jnp.exp2(m_old - m_new)
        m_ref[:, ln] = m_new
        vt = vt_ref[:, pl.ds(tile_start(j), tkv)]
        h, r = HEAD_DIM, DIFF_ACC_ROWS
        for n, e in enumerate(heads):
            ve = jnp.concatenate([vt[e * h:(e + 1) * h], ones], axis=0)
            cols = slice(2 * tq * n, 2 * tq * (n + 1))
            acc_ref[e * r:(e + 1) * r, :] = alpha[:, cols] * acc_ref[e * r:(e + 1) * r, :] + jnp.dot(
                ve, pb[:, cols], preferred_element_type=F32)

    def alive(j, heads):
        ln = lanes(heads)
        kn = kn_ref[pair, jnp.maximum(j, 0)]
        reach = qn[:, ln] * kn + sl2[:, ln] * ((j * tkv - i * tq).astype(F32) + (tkv - 1.0)) - m_ref[:, ln]
        return jnp.max(reach) > DEAD_EXP2

    def sweep(j_first, heads, watch):
        ln = lanes(heads)
        bufs = (s0_ref, s1_ref)

        def more(state):
            u, live = state
            return jnp.logical_and(DIFF_GROUP * u <= j_first, live)

        def tile_group(state):
            u, _ = state
            ja = j_first - DIFF_GROUP * u
            live = alive(ja - DIFF_GROUP, watch)
            for t in range(DIFF_GROUP):
                bufs[(t + 1) % 2][:, ln] = scores(ja - t - 1, heads)
                softmax_step(bufs[t % 2][:, ln], ja - t, False, heads)
            return u + 1, live

        assert DIFF_GROUP % 2 == 0
        groups, _ = lax.while_loop(more, tile_group, (0, alive(j_first, watch)))
        return j_first - DIFF_GROUP * groups

    both = (0, 1)
    s_diag = scores(i, both)
    s0_ref[...] = scores(i - 1, both)
    softmax_step(s_diag, i, True, both)
    j_rest = sweep(i - 1, both, (0,))
    sweep(j_rest, (1,), (1,))

    lp = lam_ref[...]
    lam = (jnp.exp(jnp.sum(lp[0:1] * lp[1:2], axis=1, keepdims=True))
           - jnp.exp(jnp.sum(lp[2:3] * lp[3:4], axis=1, keepdims=True)) + lam_init)
    acc = acc_ref[...]
    outs = []
    for e in range(2):
        a = acc[e * DIFF_ACC_ROWS:e * DIFF_ACC_ROWS + HEAD_DIM]
        inv = 1.0 / acc[e * DIFF_ACC_ROWS + HEAD_DIM:e * DIFF_ACC_ROWS + HEAD_DIM + 1]
        i1 = inv[:, 0:tq]
        i2 = inv[:, tq:]
        o = a[:, 0:tq] * i1 - lam * (a[:, tq:] * i2)
        o = o * lax.rsqrt(jnp.mean(o * o, axis=0, keepdims=True) + NORM_EPS)
        outs.append(o * gain_ref[...] * (1.0 - lam_init))
    o_ref[...] = jnp.concatenate(outs, axis=0).T.astype(BF16)


def _diff_attention(qbt, kb, vbt, knorm, lam_params, gain_col, slopes, layer, lam_init):
    s = kb.shape[0]
    tq = ATT_TQ
    npairs = N_HEADS_DIFF // 2
    return pl.pallas_call(
        functools.partial(_diff_body, lam_init),
        grid=(npairs, s // (tq * DIFF_SUB)),
        in_specs=[pl.BlockSpec(memory_space=pltpu.SMEM),
                  pl.BlockSpec((1, 1, 4 * tq), lambda p, i: (p, 0, 0)),
                  pl.BlockSpec((None, 4, DIFF_QK_DIM), lambda p, i: (layer, 0, 0)),
                  pl.BlockSpec((None, HEAD_DIM, 1), lambda p, i: (layer, 0, 0)),
                  pl.BlockSpec((LANES, tq * DIFF_SUB), lambda p, i: (p, i)),
                  pl.BlockSpec((s, LANES), lambda p, i: (0, p)),
                  pl.BlockSpec((LANES, s), lambda p, i: (p, 0))],
        out_specs=pl.BlockSpec((tq * DIFF_SUB, LANES), lambda p, i: (i, p)),
        out_shape=jax.ShapeDtypeStruct((s, N_HEADS_DIFF * HEAD_DIM), BF16),
        scratch_shapes=[pltpu.VMEM((1, 4 * tq), F32), pltpu.VMEM((2 * DIFF_ACC_ROWS, 2 * tq), F32),
                        pltpu.VMEM((ATT_TKV, 4 * tq), F32), pltpu.VMEM((ATT_TKV, 4 * tq), F32)],
        compiler_params=_params("parallel", "arbitrary"),
        name="diff_attn",
    )(knorm, slopes, lam_params, gain_col, qbt, kb, vbt)


def _sb_body(qt_ref, k_ref, vt_ref, o_ref, carry_ref, acc_ref):
    tq, tkv = ATT_TQ, ATT_TKV
    step = pl.program_id(1)
    row = lax.broadcasted_iota(jnp.int32, (LANES, tq), 0)
    ur = lax.broadcasted_iota(jnp.int32, (tkv, tkv), 0)
    uc = lax.broadcasted_iota(jnp.int32, (tkv, tkv), 1)
    upper = (uc >= ur).astype(BF16)
    kpos = lax.broadcasted_iota(jnp.int32, (tkv, 2 * tq), 0)
    qpos = lax.broadcasted_iota(jnp.int32, (tkv, 2 * tq), 1) & (tq - 1)
    causal = kpos < qpos

    carry_ref[...] = jnp.zeros(carry_ref.shape, F32)
    acc_ref[...] = jnp.zeros(acc_ref.shape, F32)
    qas = []
    for s in range(SB_SUB):
        qt = qt_ref[:, s * tq:(s + 1) * tq]
        zero = jnp.zeros_like(qt)
        qas.append(jnp.concatenate([jnp.where(row < HEAD_DIM, qt, zero), jnp.where(row >= HEAD_DIM, qt, zero)], axis=1))

    def stages(s, j, masked, guarded):
        v = {}

        def logits():
            v["ks"] = pl.multiple_of(jnp.maximum(j, 0) * tkv, tkv)
            z = jnp.dot(k_ref[pl.ds(v["ks"], tkv), :], qas[s], preferred_element_type=F32)
            v["z"] = jnp.where(j >= 0, z, NEG_BIG) if guarded else z

        def softplus():
            z = v["z"]
            sp = jnp.maximum(z, 0.0) + jnp.log(1.0 + jnp.exp2(-jnp.abs(z))) * LOG2E
            if masked:
                sp = jnp.where(causal, sp, 0.0)
            v["hi"] = sp.astype(BF16)
            v["lo"] = (sp - v["hi"].astype(F32)).astype(BF16)

        def suffix_sum():
            v["cs"] = (jnp.dot(upper, v["hi"], preferred_element_type=F32)
                       + jnp.dot(upper, v["lo"], preferred_element_type=F32))

        def weights():
            w = jnp.exp2((v["z"] + carry_ref[s]) - v["cs"])
            if masked:
                w = jnp.where(causal, w, 0.0)
            v["wb"] = w.astype(BF16)
            carry_ref[s] = carry_ref[s] - v["cs"][0:1]

        def values():
            vt = vt_ref[:, pl.ds(v["ks"], tkv)]
            h, wb = HEAD_DIM, v["wb"]
            acc_ref[s, 0:h, :] = acc_ref[s, 0:h, :] + jnp.dot(vt[0:h], wb[:, 0:tq], preferred_element_type=F32)
            acc_ref[s, h:, :] = acc_ref[s, h:, :] + jnp.dot(vt[h:], wb[:, tq:], preferred_element_type=F32)

        return (logits, softplus, suffix_sum, weights, values)

    chains = ([stages(s, SB_SUB * step + s, True, False) for s in range(SB_SUB)]
              + [stages(s, SB_SUB * step + s - 1, False, True) for s in range(SB_SUB)])
    n_stage = len(chains[0])
    for t in range(len(chains) + n_stage - 1):
        for c, chain in enumerate(chains):
            if 0 <= t - c < n_stage:
                chain[t - c]()

    for s in range(SB_SUB):
        i = SB_SUB * step + s

        def alive(s=s):
            return jnp.max(carry_ref[s]) > DEAD_EXP2

        def more(state, i=i):
            it, live = state
            return jnp.logical_and(it < i, live)

        def full_tile(state, s=s, i=i, alive=alive):
            it, _ = state
            for stage in stages(s, i - 1 - it, False, False):
                stage()
            return it + 1, alive()

        lax.while_loop(more, full_tile, (1, alive()))
        o_ref[s * tq:(s + 1) * tq, :] = acc_ref[s].T.astype(BF16)


def _sb_attention(qct, kc, vct):
    s = kc.shape[0]
    tq = ATT_TQ * SB_SUB
    npairs = N_HEADS_SB // 2
    return pl.pallas_call(
        _sb_body,
        grid=(npairs, s // tq),
        in_specs=[pl.BlockSpec((LANES, tq), lambda p, i: (p, i)),
                  pl.BlockSpec((s, LANES), lambda p, i: (0, p)),
                  pl.BlockSpec((LANES, s), lambda p, i: (p, 0))],
        out_specs=pl.BlockSpec((tq, LANES), lambda p, i: (i, p)),
        out_shape=jax.ShapeDtypeStruct((s, N_HEADS_SB * HEAD_DIM), BF16),
        scratch_shapes=[pltpu.VMEM((SB_SUB, 1, 2 * ATT_TQ), F32), pltpu.VMEM((SB_SUB, LANES, ATT_TQ), F32)],
        compiler_params=_params("parallel", "arbitrary"),
        name="sb_attn",
    )(qct, kc, vct)


def _outproj_body(x_ref, g_ref, oa_ref, ob_ref, oc_ref, w_ref, o_ref):
    mixed = jnp.concatenate([oa_ref[...], ob_ref[...], oc_ref[...]], axis=1)
    y = jnp.dot(mixed, w_ref[...], preferred_element_type=F32)
    o_ref[...] = x_ref[...] + _rms(y, g_ref[...])


def _outproj(x, gains, oa, ob, oc, wo, layer):
    s = x.shape[0]
    t = PROJ_ROWS
    nat = lambda w: pl.BlockSpec((t, w), lambda i: (i, 0))
    return pl.pallas_call(
        _outproj_body,
        grid=(s // t,),
        in_specs=[nat(D_MODEL), _picked((1, D_MODEL), layer, 3), nat(384), nat(256), nat(384),
                  _picked((D_MODEL, D_MODEL), layer)],
        out_specs=nat(D_MODEL),
        out_shape=jax.ShapeDtypeStruct((s, D_MODEL), F32),
        compiler_params=_params("parallel"),
        name="outproj",
    )(x, gains, oa, ob, oc, wo)


def _alibi_slopes(n):
    return 2.0 ** (-8.0 * jnp.arange(1, n + 1, dtype=F32) / n)


def kernel(x, norm_gains, w_ffn_gate, w_ffn_up, w_ffn_down, w_in, w_out, diff_lambda, diff_subln_gain):
    b, s, d = x.shape
    assert b == 1 and d == D_MODEL and s % DIL_CHUNK == 0 and ATT_TQ == ATT_TKV
    depth = norm_gains.shape[0]
    xs = x.reshape(s, d)

    off = [0]
    for w in IN_SIZES:
        off.append(off[-1] + w)
    col = lambda i: w_in[:, :, off[i]:off[i + 1]]

    sl_dil = jnp.repeat(_alibi_slopes(N_HEADS_DIL), HEAD_DIM).reshape(N_HEADS_DIL // 2, 1, LANES)
    sl_diff = jnp.repeat(_alibi_slopes(N_HEADS_DIFF), 2 * ATT_TQ).reshape(N_HEADS_DIFF // 2, 1, 4 * ATT_TQ)

    gains = norm_gains.reshape(depth, -1, 1, D_MODEL)
    wg = w_ffn_gate.astype(BF16)
    wu = w_ffn_up.astype(BF16)
    wd = w_ffn_down.astype(BF16)
    wn = jnp.concatenate([col(0), col(1), col(2), col(4), col(7)], axis=2).astype(BF16)
    wt = jnp.concatenate([col(3), col(5), col(6), col(8)], axis=2).transpose(0, 2, 1).astype(BF16)
    wo = w_out.astype(BF16)
    sub_gain = diff_subln_gain.reshape(depth, HEAD_DIM, 1)
    tiles_per_step = PROJ_ROWS // ATT_TKV
    assert PROJ_ROWS % ATT_TKV == 0 and tiles_per_step <= SUBLANES

    for layer in range(depth):
        lam_init = 0.8 - 0.6 * math.exp(-0.3 * layer)
        xs = _ffn(xs, gains, wg, wu, wd, layer, 0)
        dil, kb, kc, qbt, vbt, qct, vct, kn = _inproj(xs, gains, wn, wt, layer)
        knorm = kn[:, :tiles_per_step, :N_HEADS_DIFF // 2].reshape(s // ATT_TKV, N_HEADS_DIFF // 2).T
        oa = _dilated(dil, sl_dil)
        ob = _diff_attention(qbt, kb, vbt, knorm, diff_lambda, sub_gain, sl_diff, layer, lam_init)
        oc = _sb_attention(qct, kc, vct)
        xs = _outproj(xs, gains, oa, ob, oc, wo, layer)
        xs = _ffn(xs, gains, wg, wu, wd, layer, 1)
    return xs.reshape(b, s, d)
```

```python
import functools
import math

import jax
import jax.numpy as jnp
from jax import lax
from jax.experimental import pallas as pl
from jax.experimental.pallas import tpu as pltpu

D_MODEL = 1024
D_FF = 2816
HEAD_DIM = 64
N_HEADS_DIL = 6
N_HEADS_DIFF = 4
N_HEADS_SB = 6
DIFF_QK_DIM = HEAD_DIM // 2
DILATED_PAIRS = ((128, 1), (512, 4), (2048, 16))
N_BACK = 128
NORM_EPS = 1e-6
IN_SIZES = (384, 384, 384, 256, 256, 256, 384, 384, 384)

LANES = 128
SUBLANES = 8
VMEM_LIMIT = 56 * 1024 * 1024

FFN_ROWS = 512
FFN_CHUNKS = 11
PROJ_ROWS = 512
DIL_CHUNK = 2048
DIL_UNROLL = 16
ATT_TQ = 256
ATT_TKV = 256
BF16_ROWS = 16
DIFF_ACC_ROWS = HEAD_DIM + BF16_ROWS
DIFF_GROUP = 4
DIFF_SUB = 8
SB_SUB = 8
NEG_BIG = -1e30
LOG2E = 1.4426950408889634
DEAD_EXP2 = -160.0

F32 = jnp.float32
BF16 = jnp.bfloat16


def _rms(x, g):
    return x * lax.rsqrt(jnp.mean(x * x, axis=-1, keepdims=True) + NORM_EPS) * g


def _params(*sem):
    return pltpu.CompilerParams(dimension_semantics=sem, vmem_limit_bytes=VMEM_LIMIT)


def _resident(shape):
    zeros = (0,) * len(shape)
    return pl.BlockSpec(shape, lambda *_: zeros, pipeline_mode=pl.Buffered(1))


def _picked(shape, *lead):
    idx = tuple(lead) + (0,) * len(shape)
    return pl.BlockSpec((None,) * len(lead) + tuple(shape), lambda *_: idx, pipeline_mode=pl.Buffered(1))


def _ffn_body(x_ref, gpre_ref, gpost_ref, wg_ref, wu_ref, wd_ref, o_ref):
    x = x_ref[...]
    hb = _rms(x, gpre_ref[...]).astype(BF16)
    step = D_FF // FFN_CHUNKS
    y = None
    for c in range(FFN_CHUNKS):
        sl = slice(c * step, (c + 1) * step)
        g = jnp.dot(hb, wg_ref[:, sl], preferred_element_type=F32)
        u = jnp.dot(hb, wu_ref[:, sl], preferred_element_type=F32)
        a = (g / (1.0 + jnp.exp(-g)) * u).astype(BF16)
        part = jnp.dot(a, wd_ref[sl, :], preferred_element_type=F32)
        y = part if y is None else y + part
    o_ref[...] = x + 0.5 * _rms(y, gpost_ref[...])


def _ffn(x, gains, wg, wu, wd, layer, half):
    s = x.shape[0]
    row = pl.BlockSpec((FFN_ROWS, D_MODEL), lambda i: (i, 0))
    return pl.pallas_call(
        _ffn_body,
        grid=(s // FFN_ROWS,),
        in_specs=[row, _picked((1, D_MODEL), layer, 4 * half), _picked((1, D_MODEL), layer, 4 * half + 1),
                  _picked((D_MODEL, D_FF), layer, half), _picked((D_MODEL, D_FF), layer, half),
                  _picked((D_FF, D_MODEL), layer, half)],
        out_specs=row,
        out_shape=jax.ShapeDtypeStruct((s, D_MODEL), F32),
        compiler_params=_params("parallel"),
        name="ffn",
    )(x, gains, gains, wg, wu, wd)


N_NAT = 384 * 3 + 256 + 384
N_TR = 256 + 256 + 384 + 384


def _inproj_body(x_ref, g_ref, wn_ref, wt_ref, dil_ref, kb_ref, kc_ref, qbt_ref, vbt_ref, qct_ref, vct_ref,
                 kn_ref, run_ref):
    hb = _rms(x_ref[...], g_ref[...]).astype(BF16)
    pn = jnp.dot(hb, wn_ref[...], preferred_element_type=F32)
    dil_ref[:, 0:384] = pn[:, 0:384] * (HEAD_DIM ** -0.5 * LOG2E)
    dil_ref[:, 384:1152] = pn[:, 384:1152]
    kbf = pn[:, 1152:1408].astype(BF16)
    kb_ref[...] = kbf
    kc_ref[...] = pn[:, 1408:1792].astype(BF16)
    pt = lax.dot_general(wt_ref[...], hb, (((1,), (1,)), ((), ())), preferred_element_type=F32)
    qbt_ref[...] = (pt[0:256] * (DIFF_QK_DIM ** -0.5 * LOG2E)).astype(BF16)
    vbt_ref[...] = pt[256:512].astype(BF16)
    qct_ref[...] = (pt[512:896] * (HEAD_DIM ** -0.5 * LOG2E)).astype(BF16)
    vct_ref[...] = pt[896:1280].astype(BF16)

    @pl.when(pl.program_id(0) == 0)
    def _():
        run_ref[...] = jnp.zeros(run_ref.shape, F32)

    k2 = kbf.astype(F32)
    k2 = k2 * k2
    n2 = [jnp.sum(k2[:, p * LANES:(p + 1) * LANES], axis=1, keepdims=True) for p in range(N_HEADS_DIFF // 2)]
    lane = lax.broadcasted_iota(jnp.int32, (1, LANES), 1)
    rowi = lax.broadcasted_iota(jnp.int32, (SUBLANES, LANES), 0)
    run = run_ref[...]
    out = jnp.zeros((SUBLANES, LANES), F32)
    for r in range(PROJ_ROWS // ATT_TKV):
        tmax = [jnp.max(n[r * ATT_TKV:(r + 1) * ATT_TKV], axis=0, keepdims=True) for n in n2]
        run = jnp.maximum(run, jnp.where(lane == 0, tmax[0], jnp.where(lane == 1, tmax[1], 0.0)))
        out = jnp.where(rowi == r, jnp.sqrt(run), out)
    run_ref[...] = run
    kn_ref[0] = out


def _inproj(x, gains, wn, wt, layer):
    s = x.shape[0]
    t = PROJ_ROWS
    nat = lambda w: pl.BlockSpec((t, w), lambda i: (i, 0))
    tr = lambda w: pl.BlockSpec((w, t), lambda i: (0, i))
    return pl.pallas_call(
        _inproj_body,
        grid=(s // t,),
        in_specs=[nat(D_MODEL), _picked((1, D_MODEL), layer, 2), _picked((D_MODEL, N_NAT), layer),
                  _picked((N_TR, D_MODEL), layer)],
        out_specs=[nat(1152), nat(256), nat(384), tr(256), tr(256), tr(384), tr(384),
                   pl.BlockSpec((1, SUBLANES, LANES), lambda i: (i, 0, 0))],
        out_shape=[jax.ShapeDtypeStruct((s, 1152), F32),
                   jax.ShapeDtypeStruct((s, 256), BF16), jax.ShapeDtypeStruct((s, 384), BF16),
                   jax.ShapeDtypeStruct((256, s), BF16), jax.ShapeDtypeStruct((256, s), BF16),
                   jax.ShapeDtypeStruct((384, s), BF16), jax.ShapeDtypeStruct((384, s), BF16),
                   jax.ShapeDtypeStruct((s // t, SUBLANES, LANES), F32)],
        scratch_shapes=[pltpu.VMEM((1, LANES), F32)],
        compiler_params=_params("arbitrary"),
        name="inproj",
    )(x, gains, wn, wt)


def _dil_body(slope_ref, q_ref, kp_ref, kc_ref, vp_ref, vc_ref, o_ref, kbuf, vbuf, oacc, macc, lacc):
    n = pl.program_id(1)
    kbuf[0:DIL_CHUNK, :] = kp_ref[...]
    kbuf[DIL_CHUNK:, :] = kc_ref[...]
    vbuf[0:DIL_CHUNK, :] = vp_ref[...]
    vbuf[DIL_CHUNK:, :] = vc_ref[...]
    slope = slope_ref[0] * LOG2E
    lane = lax.broadcasted_iota(jnp.int32, (1, LANES), 1)
    head0 = lane < HEAD_DIM
    slope0 = jnp.max(jnp.where(head0, slope, 0.0), axis=1, keepdims=True)
    slope1 = jnp.max(jnp.where(head0, 0.0, slope), axis=1, keepdims=True)
    qi = lax.broadcasted_iota(jnp.int32, (N_BACK, 2 * N_BACK), 0)
    ki = lax.broadcasted_iota(jnp.int32, (N_BACK, 2 * N_BACK), 1)
    dist = N_BACK + qi - ki
    band = (dist >= 0) & (dist <= N_BACK)
    distf = dist.astype(F32)
    nt = (((1,), (1,)), ((), ()))

    for bi, (window, d) in enumerate(DILATED_PAIRS):
        per_chunk = DIL_CHUNK // window
        biases = [jnp.where(band, -(sl * float(d)) * distf, -jnp.inf) for sl in (slope0, slope1)]

        def tile(it, carry, d=d, window=window, per_chunk=per_chunk, bi=bi, biases=biases):
            if d == 1:
                c, r = it, 0
            elif per_chunk == 1:
                c, r = 0, it
            else:
                c, r = it // d, it % d
            q_start = c * window + r
            k_start = DIL_CHUNK + q_start - window
            if d == 1:
                rows_q = pl.ds(q_start, N_BACK)
                rows_k = pl.ds(k_start, 2 * N_BACK)
            else:
                rows_q = pl.ds(q_start, N_BACK, stride=d)
                rows_k = pl.ds(k_start, 2 * N_BACK, stride=d)
            q2 = q_ref[rows_q, :]
            k2 = kbuf[rows_k, :].astype(BF16)
            v2 = vbuf[rows_k, :].astype(BF16)
            first = jnp.logical_and(n == 0, c == 0)
            outs, ms, ls = [], [], []
            for hm, bias in zip((head0, jnp.logical_not(head0)), biases):
                qe = jnp.where(hm, q2, 0.0).astype(BF16)
                sc = lax.dot_general(qe, k2, nt, preferred_element_type=F32) + bias
                sc = jnp.concatenate([jnp.where(first, -jnp.inf, sc[:, :N_BACK]), sc[:, N_BACK:]], axis=1)
                m = jnp.max(sc, axis=1, keepdims=True)
                p = jnp.exp2(sc - m)
                ls.append(jnp.sum(p, axis=1, keepdims=True))
                ms.append(m)
                outs.append(jnp.dot(p.astype(BF16), v2, preferred_element_type=F32))
            o_blk = jnp.where(head0, outs[0], outs[1])
            m_blk = jnp.where(head0, ms[0], ms[1])
            l_blk = jnp.where(head0, ls[0], ls[1])
            if bi == 0:
                oacc[rows_q, :] = o_blk
                macc[rows_q, :] = m_blk
                lacc[rows_q, :] = l_blk
            else:
                m_old = macc[rows_q, :]
                m_new = jnp.maximum(m_old, m_blk)
                a = jnp.exp2(m_old - m_new)
                b = jnp.exp2(m_blk - m_new)
                oacc[rows_q, :] = a * oacc[rows_q, :] + b * o_blk
                lacc[rows_q, :] = a * lacc[rows_q, :] + b * l_blk
                macc[rows_q, :] = m_new
            return carry

        lax.fori_loop(0, DIL_CHUNK // N_BACK, tile, 0, unroll=DIL_UNROLL)

    o_ref[...] = (oacc[...] / lacc[...]).astype(BF16)


def _dilated(dil, slopes):
    s = dil.shape[0]
    npairs = N_HEADS_DIL // 2
    blk = lambda col0, prev: pl.BlockSpec(
        (DIL_CHUNK, LANES),
        (lambda p, n: (jnp.maximum(n - 1, 0), col0 + p)) if prev else (lambda p, n: (n, col0 + p)))
    buf = lambda rows: pltpu.VMEM((rows, LANES), F32)
    return pl.pallas_call(
        _dil_body,
        grid=(npairs, s // DIL_CHUNK),
        in_specs=[pl.BlockSpec((1, 1, LANES), lambda p, n: (p, 0, 0)),
                  blk(0, False), blk(3, True), blk(3, False), blk(6, True), blk(6, False)],
        out_specs=pl.BlockSpec((DIL_CHUNK, LANES), lambda p, n: (n, p)),
        out_shape=jax.ShapeDtypeStruct((s, N_HEADS_DIL * HEAD_DIM), BF16),
        scratch_shapes=[buf(2 * DIL_CHUNK), buf(2 * DIL_CHUNK), buf(DIL_CHUNK), buf(DIL_CHUNK), buf(DIL_CHUNK)],
        compiler_params=_params("parallel", "parallel"),
        name="dilated",
    )(slopes, dil, dil, dil, dil, dil)


def _diff_body(lam_init, kn_ref, slope_ref, lam_ref, gain_ref, qt_ref, k_ref, vt_ref, o_ref, *scratch):
    tq = ATT_TQ
    for sub in range(DIFF_SUB):
        _diff_query_tile(lam_init, DIFF_SUB * pl.program_id(1) + sub, kn_ref, slope_ref, lam_ref, gain_ref,
                         qt_ref.at[:, sub * tq:(sub + 1) * tq], k_ref, vt_ref,
                         o_ref.at[sub * tq:(sub + 1) * tq, :], *scratch)


def _diff_query_tile(lam_init, i, kn_ref, slope_ref, lam_ref, gain_ref, qt_ref, k_ref, vt_ref, o_ref,
                     m_ref, acc_ref, s0_ref, s1_ref):
    tq, tkv = ATT_TQ, ATT_TKV
    pair = pl.program_id(0)
    qt = qt_ref[...]
    row = lax.broadcasted_iota(jnp.int32, (LANES, tq), 0)
    zero = jnp.zeros_like(qt)
    top = jnp.concatenate(
        [jnp.where((row >= DIFF_QK_DIM * c) & (row < DIFF_QK_DIM * (c + 1)), qt, zero) for c in range(4)], axis=1)
    sl2 = slope_ref[0] * LOG2E
    sl_hi = sl2.astype(BF16).astype(F32)
    sl_lo = sl2 - sl_hi
    brow = lax.broadcasted_iota(jnp.int32, (LANES, 4 * tq), 0)
    bottom = jnp.where(brow == 0, sl_hi, jnp.where(brow == 1, sl_lo, 0.0)).astype(BF16)
    qa = jnp.concatenate([top, bottom], axis=0)
    prow = lax.broadcasted_iota(jnp.int32, (tkv, LANES), 0)
    pcol = lax.broadcasted_iota(jnp.int32, (tkv, LANES), 1)
    kofs = jnp.where(pcol < 2, prow, 0).astype(F32).astype(BF16)
    topf = top.astype(F32)
    qn = jnp.sqrt(jnp.sum(topf * topf, axis=0, keepdims=True))
    kpos = lax.broadcasted_iota(jnp.int32, (tkv, 4 * tq), 0)
    qpos = lax.broadcasted_iota(jnp.int32, (tkv, 4 * tq), 1) & (tq - 1)
    causal = kpos <= qpos

    m_ref[...] = jnp.full(m_ref.shape, NEG_BIG, F32)
    acc_ref[...] = jnp.zeros(acc_ref.shape, F32)
    ones = jnp.ones((DIFF_ACC_ROWS - HEAD_DIM, tkv), BF16)

    def tile_start(j):
        return pl.multiple_of(jnp.maximum(j, 0) * tkv, tkv)

    def lanes(heads):
        return slice(2 * tq * heads[0], 2 * tq * (heads[-1] + 1))

    def scores(j, heads):
        k = k_ref[pl.ds(tile_start(j), tkv), :]
        return jnp.dot(jnp.concatenate([k, kofs], axis=1), qa[:, lanes(heads)], preferred_element_type=F32)

    def softmax_step(s, j, masked, heads):
        ln = lanes(heads)
        if masked:
            s = jnp.where(causal[:, ln], s, -jnp.inf)
        c = jnp.where(j >= 0, sl2[:, ln] * (j * tkv - i * tq).astype(F32), NEG_BIG)
        m_old = m_ref[:, ln]
        m_new = jnp.maximum(m_old, jnp.max(s, axis=0, keepdims=True) + c)
        pb = jnp.exp2(s - (m_new - c)).astype(BF16)
        alpha = jnp.exp2(m_old - m_new)
        m_ref[:, ln] = m_new
        vt = vt_ref[:, pl.ds(tile_start(j), tkv)]
        h, r = HEAD_DIM, DIFF_ACC_ROWS
        for n, e in enumerate(heads):
            ve = jnp.concatenate([vt[e * h:(e + 1) * h], ones], axis=0)
            cols = slice(2 * tq * n, 2 * tq * (n + 1))
            acc_ref[e * r:(e + 1) * r, :] = alpha[:, cols] * acc_ref[e * r:(e + 1) * r, :] + jnp.dot(
                ve, pb[:, cols], preferred_element_type=F32)

    def alive(j, heads):
        ln = lanes(heads)
        kn = kn_ref[pair, jnp.maximum(j, 0)]
        reach = qn[:, ln] * kn + sl2[:, ln] * ((j * tkv - i * tq).astype(F32) + (tkv - 1.0)) - m_ref[:, ln]
        return jnp.max(reach) > DEAD_EXP2

    def sweep(j_first, heads, watch):
        ln = lanes(heads)
        bufs = (s0_ref, s1_ref)

        def more(state):
            u, live = state
            return jnp.logical_and(DIFF_GROUP * u <= j_first, live)

        def tile_group(state):
            u, _ = state
            ja = j_first - DIFF_GROUP * u
            live = alive(ja - DIFF_GROUP, watch)
            for t in range(DIFF_GROUP):
                bufs[(t + 1) % 2][:, ln] = scores(ja - t - 1, heads)
                softmax_step(bufs[t % 2][:, ln], ja - t, False, heads)
            return u + 1, live

        assert DIFF_GROUP % 2 == 0
        groups, _ = lax.while_loop(more, tile_group, (0, alive(j_first, watch)))
        return j_first - DIFF_GROUP * groups

    both = (0, 1)
    s_diag = scores(i, both)
    s0_ref[...] = scores(i - 1, both)
    softmax_step(s_diag, i, True, both)
    j_rest = sweep(i - 1, both, (0,))
    sweep(j_rest, (1,), (1,))

    lp = lam_ref[...]
    lam = (jnp.exp(jnp.sum(lp[0:1] * lp[1:2], axis=1, keepdims=True))
           - jnp.exp(jnp.sum(lp[2:3] * lp[3:4], axis=1, keepdims=True)) + lam_init)
    acc = acc_ref[...]
    outs = []
    for e in range(2):
        a = acc[e * DIFF_ACC_ROWS:e * DIFF_ACC_ROWS + HEAD_DIM]
        inv = 1.0 / acc[e * DIFF_ACC_ROWS + HEAD_DIM:e * DIFF_ACC_ROWS + HEAD_DIM + 1]
        i1 = inv[:, 0:tq]
        i2 = inv[:, tq:]
        o = a[:, 0:tq] * i1 - lam * (a[:, tq:] * i2)
        o = o * lax.rsqrt(jnp.mean(o * o, axis=0, keepdims=True) + NORM_EPS)
        outs.append(o * gain_ref[...] * (1.0 - lam_init))
    o_ref[...] = jnp.concatenate(outs, axis=0).T.astype(BF16)


def _diff_attention(qbt, kb, vbt, knorm, lam_params, gain_col, slopes, layer, lam_init):
    s = kb.shape[0]
    tq = ATT_TQ
    npairs = N_HEADS_DIFF // 2
    return pl.pallas_call(
        functools.partial(_diff_body, lam_init),
        grid=(npairs, s // (tq * DIFF_SUB)),
        in_specs=[pl.BlockSpec(memory_space=pltpu.SMEM),
                  pl.BlockSpec((1, 1, 4 * tq), lambda p, i: (p, 0, 0)),
                  pl.BlockSpec((None, 4, DIFF_QK_DIM), lambda p, i: (layer, 0, 0)),
                  pl.BlockSpec((None, HEAD_DIM, 1), lambda p, i: (layer, 0, 0)),
                  pl.BlockSpec((LANES, tq * DIFF_SUB), lambda p, i: (p, i)),
                  pl.BlockSpec((s, LANES), lambda p, i: (0, p)),
                  pl.BlockSpec((LANES, s), lambda p, i: (p, 0))],
        out_specs=pl.BlockSpec((tq * DIFF_SUB, LANES), lambda p, i: (i, p)),
        out_shape=jax.ShapeDtypeStruct((s, N_HEADS_DIFF * HEAD_DIM), BF16),
        scratch_shapes=[pltpu.VMEM((1, 4 * tq), F32), pltpu.VMEM((2 * DIFF_ACC_ROWS, 2 * tq), F32),
                        pltpu.VMEM((ATT_TKV, 4 * tq), F32), pltpu.VMEM((ATT_TKV, 4 * tq), F32)],
        compiler_params=_params("parallel", "arbitrary"),
        name="diff_attn",
    )(knorm, slopes, lam_params, gain_col, qbt, kb, vbt)


def _sb_body(qt_ref, k_ref, vt_ref, o_ref, carry_ref, acc_ref):
    tq, tkv = ATT_TQ, ATT_TKV
    step = pl.program_id(1)
    row = lax.broadcasted_iota(jnp.int32, (LANES, tq), 0)
    ur = lax.broadcasted_iota(jnp.int32, (tkv, tkv), 0)
    uc = lax.broadcasted_iota(jnp.int32, (tkv, tkv), 1)
    upper = (uc >= ur).astype(BF16)
    kpos = lax.broadcasted_iota(jnp.int32, (tkv, 2 * tq), 0)
    qpos = lax.broadcasted_iota(jnp.int32, (tkv, 2 * tq), 1) & (tq - 1)
    causal = kpos < qpos

    carry_ref[...] = jnp.zeros(carry_ref.shape, F32)
    acc_ref[...] = jnp.zeros(acc_ref.shape, F32)
    qas = []
    for s in range(SB_SUB):
        qt = qt_ref[:, s * tq:(s + 1) * tq]
        zero = jnp.zeros_like(qt)
        qas.append(jnp.concatenate([jnp.where(row < HEAD_DIM, qt, zero), jnp.where(row >= HEAD_DIM, qt, zero)], axis=1))

    def stages(s, j, masked, guarded):
        v = {}

        def logits():
            v["ks"] = pl.multiple_of(jnp.maximum(j, 0) * tkv, tkv)
            z = jnp.dot(k_ref[pl.ds(v["ks"], tkv), :], qas[s], preferred_element_type=F32)
            v["z"] = jnp.where(j >= 0, z, NEG_BIG) if guarded else z

        def softplus():
            z = v["z"]
            sp = jnp.maximum(z, 0.0) + jnp.log(1.0 + jnp.exp2(-jnp.abs(z))) * LOG2E
            if masked:
                sp = jnp.where(causal, sp, 0.0)
            v["hi"] = sp.astype(BF16)
            v["lo"] = (sp - v["hi"].astype(F32)).astype(BF16)

        def suffix_sum():
            v["cs"] = (jnp.dot(upper, v["hi"], preferred_element_type=F32)
                       + jnp.dot(upper, v["lo"], preferred_element_type=F32))

        def weights():
            w = jnp.exp2((v["z"] + carry_ref[s]) - v["cs"])
            if masked:
                w = jnp.where(causal, w, 0.0)
            v["wb"] = w.astype(BF16)
            carry_ref[s] = carry_ref[s] - v["cs"][0:1]

        def values():
            vt = vt_ref[:, pl.ds(v["ks"], tkv)]
            h, wb = HEAD_DIM, v["wb"]
            acc_ref[s, 0:h, :] = acc_ref[s, 0:h, :] + jnp.dot(vt[0:h], wb[:, 0:tq], preferred_element_type=F32)
            acc_ref[s, h:, :] = acc_ref[s, h:, :] + jnp.dot(vt[h:], wb[:, tq:], preferred_element_type=F32)

        return (logits, softplus, suffix_sum, weights, values)

    chains = ([stages(s, SB_SUB * step + s, True, False) for s in range(SB_SUB)]
              + [stages(s, SB_SUB * step + s - 1, False, True) for s in range(SB_SUB)])
    n_stage = len(chains[0])
    for t in range(len(chains) + n_stage - 1):
        for c, chain in enumerate(chains):
            if 0 <= t - c < n_stage:
                chain[t - c]()

    for s in range(SB_SUB):
        i = SB_SUB * step + s

        def alive(s=s):
            return jnp.max(carry_ref[s]) > DEAD_EXP2

        def more(state, i=i):
            it, live = state
            return jnp.logical_and(it < i, live)

        def full_tile(state, s=s, i=i, alive=alive):
            it, _ = state
            for stage in stages(s, i - 1 - it, False, False):
                stage()
            return it + 1, alive()

        lax.while_loop(more, full_tile, (1, alive()))
        o_ref[s * tq:(s + 1) * tq, :] = acc_ref[s].T.astype(BF16)


def _sb_attention(qct, kc, vct):
    s = kc.shape[0]
    tq = ATT_TQ * SB_SUB
    npairs = N_HEADS_SB // 2
    return pl.pallas_call(
        _sb_body,
        grid=(npairs, s // tq),
        in_specs=[pl.BlockSpec((LANES, tq), lambda p, i: (p, i)),
                  pl.BlockSpec((s, LANES), lambda p, i: (0, p)),
                  pl.BlockSpec((LANES, s), lambda p, i: (p, 0))],
        out_specs=pl.BlockSpec((tq, LANES), lambda p, i: (i, p)),
        out_shape=jax.ShapeDtypeStruct((s, N_HEADS_SB * HEAD_DIM), BF16),
        scratch_shapes=[pltpu.VMEM((SB_SUB, 1, 2 * ATT_TQ), F32), pltpu.VMEM((SB_SUB, LANES, ATT_TQ), F32)],
        compiler_params=_params("parallel", "arbitrary"),
        name="sb_attn",
    )(qct, kc, vct)


def _outproj_body(x_ref, g_ref, oa_ref, ob_ref, oc_ref, w_ref, o_ref):
    mixed = jnp.concatenate([oa_ref[...], ob_ref[...], oc_ref[...]], axis=1)
    y = jnp.dot(mixed, w_ref[...], preferred_element_type=F32)
    o_ref[...] = x_ref[...] + _rms(y, g_ref[...])


def _outproj(x, gains, oa, ob, oc, wo, layer):
    s = x.shape[0]
    t = PROJ_ROWS
    nat = lambda w: pl.BlockSpec((t, w), lambda i: (i, 0))
    return pl.pallas_call(
        _outproj_body,
        grid=(s // t,),
        in_specs=[nat(D_MODEL), _picked((1, D_MODEL), layer, 3), nat(384), nat(256), nat(384),
                  _picked((D_MODEL, D_MODEL), layer)],
        out_specs=nat(D_MODEL),
        out_shape=jax.ShapeDtypeStruct((s, D_MODEL), F32),
        compiler_params=_params("parallel"),
        name="outproj",
    )(x, gains, oa, ob, oc, wo)


def _alibi_slopes(n):
    return 2.0 ** (-8.0 * jnp.arange(1, n + 1, dtype=F32) / n)


def kernel(x, norm_gains, w_ffn_gate, w_ffn_up, w_ffn_down, w_in, w_out, diff_lambda, diff_subln_gain):
    b, s, d = x.shape
    assert b == 1 and d == D_MODEL and s % DIL_CHUNK == 0 and ATT_TQ == ATT_TKV
    depth = norm_gains.shape[0]
    xs = x.reshape(s, d)

    off = [0]
    for w in IN_SIZES:
        off.append(off[-1] + w)
    col = lambda i: w_in[:, :, off[i]:off[i + 1]]

    sl_dil = jnp.repeat(_alibi_slopes(N_HEADS_DIL), HEAD_DIM).reshape(N_HEADS_DIL // 2, 1, LANES)
    sl_diff = jnp.repeat(_alibi_slopes(N_HEADS_DIFF), 2 * ATT_TQ).reshape(N_HEADS_DIFF // 2, 1, 4 * ATT_TQ)

    gains = norm_gains.reshape(depth, -1, 1, D_MODEL)
    wg = w_ffn_gate.astype(BF16)
    wu = w_ffn_up.astype(BF16)
    wd = w_ffn_down.astype(BF16)
    wn = jnp.concatenate([col(0), col(1), col(2), col(4), col(7)], axis=2).astype(BF16)
    wt = jnp.concatenate([col(3), col(5), col(6), col(8)], axis=2).transpose(0, 2, 1).astype(BF16)
    wo = w_out.astype(BF16)
    sub_gain = diff_subln_gain.reshape(depth, HEAD_DIM, 1)
    tiles_per_step = PROJ_ROWS // ATT_TKV
    assert PROJ_ROWS % ATT_TKV == 0 and tiles_per_step <= SUBLANES

    for layer in range(depth):
        lam_init = 0.8 - 0.6 * math.exp(-0.3 * layer)
        xs = _ffn(xs, gains, wg, wu, wd, layer, 0)
        dil, kb, kc, qbt, vbt, qct, vct, kn = _inproj(xs, gains, wn, wt, layer)
        knorm = kn[:, :tiles_per_step, :N_HEADS_DIFF // 2].reshape(s // ATT_TKV, N_HEADS_DIFF // 2).T
        oa = _dilated(dil, sl_dil)
        ob = _diff_attention(qbt, kb, vbt, knorm, diff_lambda, sub_gain, sl_diff, layer, lam_init)
        oc = _sb_attention(qct, kc, vct)
        xs = _outproj(xs, gains, oa, ob, oc, wo, layer)
        xs = _ffn(xs, gains, wg, wu, wd, layer, 1)
    return xs.reshape(b, s, d)
```
